```python
import math
import jax, jax.numpy as jnp
from jax import lax
import numpy as np

D_MODEL = 2048
BATCH = 4
SEQ = 4096
DEPTH = 1
DEC_BATCH = 1
DEC_SEQ = 16384
PAST_LEN = 128

MIX_W = D_MODEL
SSM_W = MIX_W // 2
ATTN_W = MIX_W - SSM_W
SSM_H = 16
SSM_G = SSM_W // SSM_H
SSM_P = 64
HEAD_DIM = 128
N_HEADS = ATTN_W // HEAD_DIM
N_KV_HEADS = 2
GQA_GROUP = N_HEADS // N_KV_HEADS
KV_W = N_KV_HEADS * HEAD_DIM
IN_W = SSM_W + ATTN_W + 2 * KV_W
D_FF = 4 * D_MODEL
GRID_W = 64
AXIS_DIM = HEAD_DIM // 2
ROPE_THETA = 10000.0
Q_BLOCK = 128
EPS = 1e-6
DT_MIN = 0.001
DT_MAX = 0.1

kernel_name = "hymba_s5_gqa_axial_rope_encoder"


def _rms_norm(x, g):
    xf = x.astype(jnp.float32)
    y = xf * lax.rsqrt(jnp.mean(xf * xf, axis=-1, keepdims=True) + EPS)
    return (y * g.astype(jnp.float32)).astype(x.dtype)


def _grid_positions(length):
    rows = length // GRID_W
    row = jnp.repeat(jnp.arange(rows, dtype=jnp.float32), GRID_W)
    col = jnp.tile(jnp.arange(GRID_W, dtype=jnp.float32), rows)
    return row, col


def _rope_axis(x, pos):
    inv_freq = ROPE_THETA ** (-jnp.arange(0, AXIS_DIM, 2, dtype=jnp.float32) / AXIS_DIM)
    ang = pos[:, None] * inv_freq[None, :]
    cos = jnp.cos(ang)[None, :, None, :].astype(x.dtype)
    sin = jnp.sin(ang)[None, :, None, :].astype(x.dtype)
    half = AXIS_DIM // 2
    x1, x2 = x[..., :half], x[..., half:]
    return jnp.concatenate([x1 * cos - x2 * sin, x2 * cos + x1 * sin], axis=-1)


def _axial_rope(x, row, col):
    return jnp.concatenate([_rope_axis(x[..., :AXIS_DIM], row),
                            _rope_axis(x[..., AXIS_DIM:], col)], axis=-1)


def _scan_combine(e1, e2):
    a1r, a1i, b1r, b1i = e1
    a2r, a2i, b2r, b2i = e2
    return (a2r * a1r - a2i * a1i,
            a2r * a1i + a2i * a1r,
            a2r * b1r - a2i * b1i + b2r,
            a2r * b1i + a2i * b1r + b2i)


def _s5_direction(u, a_re, a_im, log_dt, b_re, b_im, c_re, c_im, reverse):
    f32 = jnp.float32
    lr, li = a_re.astype(f32), a_im.astype(f32)
    dt = jnp.exp(log_dt.astype(f32))[:, None]
    mag = jnp.exp(lr * dt)
    lb_re, lb_im = mag * jnp.cos(li * dt), mag * jnp.sin(li * dt)
    den = lr * lr + li * li
    nr, ni = lb_re - 1.0, lb_im
    w_re = (nr * lr + ni * li) / den
    w_im = (ni * lr - nr * li) / den
    br, bi = b_re.astype(f32), b_im.astype(f32)
    bb_re = w_re[..., None] * br - w_im[..., None] * bi
    bb_im = w_re[..., None] * bi + w_im[..., None] * br
    bu_re = jnp.einsum('blgh,gph->blgp', u, bb_re)
    bu_im = jnp.einsum('blgh,gph->blgp', u, bb_im)
    ar = jnp.broadcast_to(lb_re, bu_re.shape)
    ai = jnp.broadcast_to(lb_im, bu_im.shape)
    _, _, s_re, s_im = lax.associative_scan(_scan_combine, (ar, ai, bu_re, bu_im),
                                            reverse=reverse, axis=1)
    return (jnp.einsum('blgp,ghp->blgh', s_re, c_re.astype(f32))
            - jnp.einsum('blgp,ghp->blgh', s_im, c_im.astype(f32)))


def _s5_mixer(u, ssm_a_re, ssm_a_im, ssm_log_dt, ssm_b_re, ssm_b_im, ssm_c_re, ssm_c_im, ssm_d, w_glu):
    b, L, _ = u.shape
    uf = u.astype(jnp.float32).reshape(b, L, SSM_G, SSM_H)
    y = (_s5_direction(uf, ssm_a_re[0], ssm_a_im[0], ssm_log_dt[0], ssm_b_re[0], ssm_b_im[0],
                       ssm_c_re[0], ssm_c_im[0], reverse=False)
         + _s5_direction(uf, ssm_a_re[1], ssm_a_im[1], ssm_log_dt[1], ssm_b_re[1], ssm_b_im[1],
                         ssm_c_re[1], ssm_c_im[1], reverse=True)
         + ssm_d.astype(jnp.float32).reshape(SSM_G, SSM_H) * uf)
    y = jax.nn.gelu(y.reshape(b, L, SSM_W)).astype(u.dtype)
    return y * jax.nn.sigmoid(y @ w_glu)


def _attention_mixer(q, k, v, q_norm, k_norm):
    b, L, _ = q.shape
    row, col = _grid_positions(L)
    q = _rms_norm(q.reshape(b, L, N_HEADS, HEAD_DIM), q_norm)
    k = _rms_norm(k.reshape(b, L, N_KV_HEADS, HEAD_DIM), k_norm)
    v = v.reshape(b, L, N_KV_HEADS, HEAD_DIM)
    q = _axial_rope(q, row, col) * jnp.asarray(1.0 / math.sqrt(HEAD_DIM), dtype=q.dtype)
    k = _axial_rope(k, row, col)
    nblk = L // Q_BLOCK
    qb = q.reshape(b, nblk, Q_BLOCK, N_KV_HEADS, GQA_GROUP, HEAD_DIM).transpose(1, 0, 2, 3, 4, 5)

    def one_block(qi):
        s = jnp.einsum('bqkgd,bskd->bkgqs', qi, k).astype(jnp.float32)
        p = jax.nn.softmax(s, axis=-1).astype(v.dtype)
        return jnp.einsum('bkgqs,bskd->bqkgd', p, v)

    o = lax.map(one_block, qb)
    return o.transpose(1, 0, 2, 3, 4, 5).reshape(b, L, ATTN_W)


def _layer(x, pre_mix_norm, w_in, ssm_a_re, ssm_a_im, ssm_log_dt, ssm_b_re, ssm_b_im,
           ssm_c_re, ssm_c_im, ssm_d, w_glu, q_norm, k_norm, ssm_out_norm, attn_out_norm,
           w_out, post_mix_norm, pre_mlp_norm, w_up, w_down, post_mlp_norm):
    h = _rms_norm(x, pre_mix_norm)
    proj = h @ w_in
    u = proj[..., :SSM_W]
    q = proj[..., SSM_W:SSM_W + ATTN_W]
    k = proj[..., SSM_W + ATTN_W:SSM_W + ATTN_W + KV_W]
    v = proj[..., SSM_W + ATTN_W + KV_W:]
    y_ssm = _s5_mixer(u, ssm_a_re, ssm_a_im, ssm_log_dt, ssm_b_re, ssm_b_im,
                      ssm_c_re, ssm_c_im, ssm_d, w_glu)
    y_att = _attention_mixer(q, k, v, q_norm, k_norm)
    mixed = jnp.concatenate([_rms_norm(y_ssm, ssm_out_norm),
                             _rms_norm(y_att, attn_out_norm)], axis=-1) @ w_out
    x = x + _rms_norm(mixed, post_mix_norm)
    h = _rms_norm(x, pre_mlp_norm)
    m = jnp.square(jax.nn.relu(h @ w_up)) @ w_down
    return x + _rms_norm(m, post_mlp_norm)


def setup_inputs(seed: int = 0) -> dict:
    key = jax.random.key(seed)
    ks = jax.random.split(key, 32)
    f32 = jnp.float32
    nrm = lambda k, shape, s: jax.random.normal(k, shape, f32) * s
    gain = lambda k, n: 1.0 + 0.02 * jax.random.normal(k, (DEPTH, n), f32)
    a_im_base = jnp.pi * jnp.arange(SSM_P, dtype=f32)
    return {
        "x_prompt": jax.random.normal(ks[0], (BATCH, SEQ, D_MODEL), f32),
        "x_sample": jax.random.normal(ks[1], (DEC_BATCH, DEC_SEQ, D_MODEL), f32),
        "pre_mix_norm": gain(ks[2], D_MODEL),
        "w_in": nrm(ks[3], (DEPTH, D_MODEL, IN_W), D_MODEL ** -0.5),
        "ssm_a_re": -0.5 + 0.01 * jax.random.normal(ks[4], (DEPTH, 2, SSM_G, SSM_P), f32),
        "ssm_a_im": a_im_base + 0.01 * jax.random.normal(ks[5], (DEPTH, 2, SSM_G, SSM_P), f32),
        "ssm_log_dt": jax.random.uniform(ks[6], (DEPTH, 2, SSM_G), f32,
                                         minval=math.log(DT_MIN), maxval=math.log(DT_MAX)),
        "ssm_b_re": nrm(ks[7], (DEPTH, 2, SSM_G, SSM_P, SSM_H), (0.5 / SSM_H) ** 0.5),
        "ssm_b_im": nrm(ks[8], (DEPTH, 2, SSM_G, SSM_P, SSM_H), (0.5 / SSM_H) ** 0.5),
        "ssm_c_re": nrm(ks[9], (DEPTH, 2, SSM_G, SSM_H, SSM_P), (0.5 / SSM_P) ** 0.5),
        "ssm_c_im": nrm(ks[10], (DEPTH, 2, SSM_G, SSM_H, SSM_P), (0.5 / SSM_P) ** 0.5),
        "ssm_d": nrm(ks[11], (DEPTH, SSM_W), 1.0),
        "w_glu": nrm(ks[12], (DEPTH, SSM_W, SSM_W), SSM_W ** -0.5),
        "q_norm": gain(ks[13], HEAD_DIM),
        "k_norm": gain(ks[14], HEAD_DIM),
        "ssm_out_norm": gain(ks[15], SSM_W),
        "attn_out_norm": gain(ks[16], ATTN_W),
        "w_out": nrm(ks[17], (DEPTH, MIX_W, D_MODEL), MIX_W ** -0.5),
        "post_mix_norm": gain(ks[18], D_MODEL),
        "pre_mlp_norm": gain(ks[19], D_MODEL),
        "w_up": nrm(ks[20], (DEPTH, D_MODEL, D_FF), D_MODEL ** -0.5),
        "w_down": nrm(ks[21], (DEPTH, D_FF, D_MODEL), D_FF ** -0.5),
        "post_mlp_norm": gain(ks[22], D_MODEL),
    }


def reference(x_prompt, x_sample, pre_mix_norm, w_in, ssm_a_re, ssm_a_im, ssm_log_dt,
              ssm_b_re, ssm_b_im, ssm_c_re, ssm_c_im, ssm_d, w_glu, q_norm, k_norm,
              ssm_out_norm, attn_out_norm, w_out, post_mix_norm, pre_mlp_norm, w_up,
              w_down, post_mlp_norm):
    y_prompt = x_prompt
    y_sample = x_sample
    for l in range(DEPTH):
        p = (pre_mix_norm[l], w_in[l], ssm_a_re[l], ssm_a_im[l], ssm_log_dt[l],
             ssm_b_re[l], ssm_b_im[l], ssm_c_re[l], ssm_c_im[l], ssm_d[l], w_glu[l],
             q_norm[l], k_norm[l], ssm_out_norm[l], attn_out_norm[l], w_out[l],
             post_mix_norm[l], pre_mlp_norm[l], w_up[l], w_down[l], post_mlp_norm[l])
        y_prompt = _layer(y_prompt, *p)
        y_sample = _layer(y_sample, *p)
    return (y_prompt, y_sample)
```

```python
import functools
import math

import jax
import jax.numpy as jnp
from jax import lax
from jax.experimental import pallas as pl
from jax.experimental.pallas import tpu as pltpu

F32 = jnp.float32
BF16 = jnp.bfloat16

SSM_H = 16
SSM_P = 64
HEAD_DIM = 128
N_KV_HEADS = 2
GRID_W = 64
ROPE_THETA = 10000.0
EPS = 1e-6

LANES = 128
SUBLANES = 8
VMEM_LIMIT_BYTES = 56 * 1024 * 1024

TOKEN_TILE = 512
FF_TILE = 512
GROUPS_PER_BLOCK = LANES // SSM_H
SEG_LEN = 64
SCAN_CHUNK = SUBLANES * SEG_LEN
Q_TILE = 512
KV_TILE = 1024


def _params(*semantics):
    return pltpu.CompilerParams(dimension_semantics=semantics, vmem_limit_bytes=VMEM_LIMIT_BYTES)


def _rms(x, g):
    return x * lax.rsqrt(jnp.mean(x * x, axis=-1, keepdims=True) + EPS) * g


def _inproj_kernel(x_ref, g_ref, w_ref, cos_ref, sin_ref, qg_ref, kg_ref,
                   u_ref, q_ref, k_ref, v_ref, *, ssm_w, attn_w, kv_w):
    h = _rms(x_ref[0], g_ref[...]).astype(BF16)

    def proj(lo, width):
        return jnp.dot(h, w_ref[:, lo:lo + width], preferred_element_type=F32)

    u_ref[0] = proj(0, ssm_w).astype(BF16)
    v_ref[0] = proj(ssm_w + attn_w + kv_w, kv_w).astype(BF16)

    cos = cos_ref[...]
    sin = sin_ref[...]
    lane = lax.broadcasted_iota(jnp.int32, cos.shape, 1)
    first_half = (lane & (HEAD_DIM // 4)) == 0

    def norm_rope(xh, gain):
        n = _rms(xh, gain)
        partner = jnp.where(first_half,
                            pltpu.roll(n, HEAD_DIM - HEAD_DIM // 4, 1),
                            pltpu.roll(n, HEAD_DIM // 4, 1))
        return n * cos + partner * sin

    scale = 1.0 / math.sqrt(HEAD_DIM)
    for hh in range(attn_w // HEAD_DIM):
        qh = proj(ssm_w + hh * HEAD_DIM, HEAD_DIM)
        q_ref[0, :, hh * HEAD_DIM:(hh + 1) * HEAD_DIM] = (norm_rope(qh, qg_ref[...]) * scale).astype(BF16)
    for hh in range(kv_w // HEAD_DIM):
        kh = proj(ssm_w + attn_w + hh * HEAD_DIM, HEAD_DIM)
        k_ref[0, :, hh * HEAD_DIM:(hh + 1) * HEAD_DIM] = norm_rope(kh, kg_ref[...]).astype(BF16)


def _in_projection(x, gain, w_in, cos_tab, sin_tab, q_gain, k_gain, ssm_w, attn_w, kv_w):
    b, L, d = x.shape
    tm = min(TOKEN_TILE, L)
    grid = (b, L // tm)
    tok = lambda width: pl.BlockSpec((1, tm, width), lambda bi, ti: (bi, ti, 0))
    full = lambda a: pl.BlockSpec(a.shape, lambda bi, ti: (0,) * a.ndim)
    tab = pl.BlockSpec((tm, HEAD_DIM), lambda bi, ti: (ti, 0))
    return pl.pallas_call(
        functools.partial(_inproj_kernel, ssm_w=ssm_w, attn_w=attn_w, kv_w=kv_w),
        grid=grid,
        in_specs=[tok(d), full(gain), full(w_in), tab, tab, full(q_gain), full(k_gain)],
        out_specs=[tok(ssm_w), tok(attn_w), tok(kv_w), tok(kv_w)],
        out_shape=[jax.ShapeDtypeStruct((b, L, ssm_w), BF16),
                   jax.ShapeDtypeStruct((b, L, attn_w), BF16),
                   jax.ShapeDtypeStruct((b, L, kv_w), BF16),
                   jax.ShapeDtypeStruct((b, L, kv_w), BF16)],
        compiler_params=_params("parallel", "parallel"),
        name="in_projection",
    )(x, gain, w_in, cos_tab, sin_tab, q_gain, k_gain)


def _rope_tables(L):
    axis_dim = HEAD_DIM // 2
    rows = L // GRID_W
    inv_freq = ROPE_THETA ** (-jnp.arange(0, axis_dim, 2, dtype=F32) / axis_dim)
    ang_r = jnp.arange(rows, dtype=F32)[:, None] * inv_freq[None, :]
    ang_c = jnp.arange(GRID_W, dtype=F32)[:, None] * inv_freq[None, :]
    rep = lambda t: jnp.repeat(t, GRID_W, axis=0)
    til = lambda t: jnp.tile(t, (rows, 1))
    cr, sr, cc, sc = rep(jnp.cos(ang_r)), rep(jnp.sin(ang_r)), til(jnp.cos(ang_c)), til(jnp.sin(ang_c))
    return (jnp.concatenate([cr, cr, cc, cc], axis=1),
            jnp.concatenate([-sr, sr, -sc, sc], axis=1))


def _zoh(lr, li, dt):
    mag = jnp.exp(lr * dt)
    lb_re, lb_im = mag * jnp.cos(li * dt), mag * jnp.sin(li * dt)
    den = lr * lr + li * li
    nr, ni = lb_re - 1.0, lb_im
    return lb_re, lb_im, (nr * lr + ni * li) / den, (ni * lr - nr * li) / den


def _s5_prep_kernel(lr_ref, li_ref, ldt_ref, lrx_ref, lix_ref, ldtx_ref, br_ref, bi_ref,
                    lam_re_ref, lam_im_ref, lamseg_re_ref, lamseg_im_ref, bb_re_ref, bb_im_ref, *, seg_len):
    lb_re, lb_im, _, _ = _zoh(lr_ref[...], li_ref[...], jnp.exp(ldt_ref[...]))
    lam_re_ref[...] = lb_re
    lam_im_ref[...] = lb_im
    pr, pi = lb_re, lb_im
    for _ in range(int(math.log2(seg_len))):
        pr, pi = pr * pr - pi * pi, 2.0 * pr * pi
    lamseg_re_ref[...] = pr
    lamseg_im_ref[...] = pi
    _, _, w_re, w_im = _zoh(lrx_ref[...], lix_ref[...], jnp.exp(ldtx_ref[...]))
    br, bi = br_ref[...], bi_ref[...]
    bb_re_ref[...] = w_re * br - w_im * bi
    bb_im_ref[...] = w_re * bi + w_im * br


def _s5_discretise(a_re, a_im, log_dt, b_re, b_im, seg_len):
    two, G, P = a_re.shape
    H = b_re.shape[-1]
    rows = two * G
    small = lambda a: a.reshape(rows, P)
    wide = lambda a: jnp.repeat(a.reshape(rows, P), H, axis=1)
    ldt = jnp.broadcast_to(log_dt.reshape(rows, 1), (rows, P))
    args = (small(a_re), small(a_im), ldt, wide(a_re), wide(a_im), jnp.repeat(ldt, H, axis=1),
            b_re.reshape(rows, P * H), b_im.reshape(rows, P * H))
    sds = lambda w: jax.ShapeDtypeStruct((rows, w), F32)
    return pl.pallas_call(
        functools.partial(_s5_prep_kernel, seg_len=seg_len),
        out_shape=[sds(P)] * 4 + [sds(P * H)] * 2,
        name="s5_discretise",
    )(*args)


def _block_diag(m, per_block):
    n, r, c = m.shape
    nb = n // per_block
    eye = jnp.eye(per_block, dtype=m.dtype)
    out = m.reshape(nb, per_block, r, 1, c) * eye[None, :, None, :, None]
    return out.reshape(nb, per_block * r, per_block * c)


def _s5_scan_kernel(uf_ref, ub_ref, bre_ref, bim_ref, cre_ref, cim_ref,
                    lre_ref, lim_ref, sre_ref, sim_ref,
                    yf_ref, yb_ref,
                    xfr, xfi, xbr, xbi, car_ref, *, seg_len):
    @pl.when(pl.program_id(2) == 0)
    def _():
        car_ref[...] = jnp.zeros_like(car_ref)

    uf = uf_ref[0]
    ub = ub_ref[0]
    xfr[...] = jnp.dot(uf, bre_ref[0, 0], preferred_element_type=F32)
    xfi[...] = jnp.dot(uf, bim_ref[0, 0], preferred_element_type=F32)
    xbr[...] = jnp.dot(ub, bre_ref[1, 0], preferred_element_type=F32)
    xbi[...] = jnp.dot(ub, bim_ref[1, 0], preferred_element_type=F32)

    lfr, lfi = lre_ref[0, 0], lim_ref[0, 0]
    lbr, lbi = lre_ref[1, 0], lim_ref[1, 0]

    def step(t, carry, store):
        fr, fi, br, bi = carry
        rf = pl.ds(pl.multiple_of(t * SUBLANES, SUBLANES), SUBLANES)
        rb = pl.ds(pl.multiple_of((seg_len - 1 - t) * SUBLANES, SUBLANES), SUBLANES)
        fr, fi = lfr * fr - lfi * fi + xfr[rf, :], lfr * fi + lfi * fr + xfi[rf, :]
        br, bi = lbr * br - lbi * bi + xbr[rb, :], lbr * bi + lbi * br + xbi[rb, :]
        if store:
            xfr[rf, :] = fr
            xfi[rf, :] = fi
            xbr[rb, :] = br
            xbi[rb, :] = bi
        return fr, fi, br, bi

    zero = jnp.zeros_like(lfr)
    efr, efi, ebr, ebi = lax.fori_loop(0, seg_len, functools.partial(step, store=False),
                                       (zero, zero, zero, zero), unroll=2)

    sfr, sfi = sre_ref[0, 0, 0:1], sim_ref[0, 0, 0:1]
    sbr, sbi = sre_ref[1, 0, 0:1], sim_ref[1, 0, 0:1]
    cr, ci = car_ref[0:1], car_ref[1:2]
    rows_r, rows_i = [], []
    for j in range(SUBLANES):
        rows_r.append(cr)
        rows_i.append(ci)
        cr, ci = sfr * cr - sfi * ci + efr[j:j + 1], sfr * ci + sfi * cr + efi[j:j + 1]
    car_ref[0:1] = cr
    car_ref[1:2] = ci
    in_fr, in_fi = jnp.concatenate(rows_r, axis=0), jnp.concatenate(rows_i, axis=0)
    cr, ci = car_ref[2:3], car_ref[3:4]
    rows_r, rows_i = [None] * SUBLANES, [None] * SUBLANES
    for j in reversed(range(SUBLANES)):
        rows_r[j] = cr
        rows_i[j] = ci
        cr, ci = sbr * cr - sbi * ci + ebr[j:j + 1], sbr * ci + sbi * cr + ebi[j:j + 1]
    car_ref[2:3] = cr
    car_ref[3:4] = ci
    in_br, in_bi = jnp.concatenate(rows_r, axis=0), jnp.concatenate(rows_i, axis=0)

    lax.fori_loop(0, seg_len, functools.partial(step, store=True), (in_fr, in_fi, in_br, in_bi), unroll=2)

    def readout(sr, si, d):
        y = (jnp.dot(sr[...].astype(BF16), cre_ref[d, 0], preferred_element_type=F32)
             - jnp.dot(si[...].astype(BF16), cim_ref[d, 0], preferred_element_type=F32))
        return y.astype(BF16)

    yf_ref[0] = readout(xfr, xfi, 0)
    yb_ref[0] = readout(xbr, xbi, 1)


def _s5_scan(u_perm, b_re, b_im, c_re, c_im, lam_re, lam_im, lamseg_re, lamseg_im, seg_len):
    b, L, W = u_perm.shape
    chunk = SUBLANES * seg_len
    nblk, nchunk = W // LANES, L // chunk
    n_state = b_re.shape[-1]
    grid = (b, nblk, nchunk)
    fwd = pl.BlockSpec((1, chunk, LANES), lambda bi, gi, ci: (bi, ci, gi))
    bwd = pl.BlockSpec((1, chunk, LANES), lambda bi, gi, ci: (bi, nchunk - 1 - ci, gi))
    wspec = lambda a: pl.BlockSpec((2, 1) + a.shape[2:], lambda bi, gi, ci: (0, gi, 0, 0))
    out = jax.ShapeDtypeStruct((b, L, W), BF16)
    state = pltpu.VMEM((chunk, n_state), F32)
    return pl.pallas_call(
        functools.partial(_s5_scan_kernel, seg_len=seg_len),
        grid=grid,
        in_specs=[fwd, bwd, wspec(b_re), wspec(b_im), wspec(c_re), wspec(c_im),
                  wspec(lam_re), wspec(lam_im), wspec(lamseg_re), wspec(lamseg_im)],
        out_specs=[fwd, bwd],
        out_shape=[out, out],
        scratch_shapes=[state, state, state, state, pltpu.VMEM((SUBLANES, n_state), F32)],
        compiler_params=_params("parallel", "parallel", "arbitrary"),
        name="s5_scan",
    )(u_perm, u_perm, b_re, b_im, c_re, c_im, lam_re, lam_im, lamseg_re, lamseg_im)


def _glu_kernel(yf_ref, yb_ref, u_ref, d_ref, w_ref, g_ref, o_ref):
    y = yf_ref[0].astype(F32) + yb_ref[0].astype(F32) + d_ref[...] * u_ref[0].astype(F32)
    z = jax.nn.gelu(y)
    gate = jnp.dot(z.astype(BF16), w_ref[...], preferred_element_type=F32)
    o_ref[0] = _rms(z * jax.nn.sigmoid(gate), g_ref[...]).astype(BF16)


def _glu(yf, yb, u, d, w_glu, gain):
    b, L, W = u.shape
    tm = min(TOKEN_TILE, L)
    tok = pl.BlockSpec((1, tm, W), lambda bi, ti: (bi, ti, 0))
    full = lambda a: pl.BlockSpec(a.shape, lambda bi, ti: (0,) * a.ndim)
    return pl.pallas_call(
        _glu_kernel,
        grid=(b, L // tm),
        in_specs=[tok, tok, tok, full(d), full(w_glu), full(gain)],
        out_specs=tok,
        out_shape=jax.ShapeDtypeStruct((b, L, W), BF16),
        compiler_params=_params("parallel", "parallel"),
        name="ssm_glu",
    )(yf, yb, u, d, w_glu, gain)


def _attn_kernel(q_ref, kt_ref, v_ref, o_ref, m_ref, l_ref, acc_ref, *, group):
    ki = pl.program_id(3)

    @pl.when(ki == 0)
    def _():
        m_ref[...] = jnp.full_like(m_ref, -jnp.inf)
        l_ref[...] = jnp.zeros_like(l_ref)
        acc_ref[...] = jnp.zeros_like(acc_ref)

    kt = kt_ref[0]
    v = v_ref[0]
    for h in range(group):
        q = q_ref[0, :, h * HEAD_DIM:(h + 1) * HEAD_DIM]
        s = jnp.dot(q, kt, preferred_element_type=F32)
        m_prev = m_ref[h]
        m_new = jnp.maximum(m_prev, jnp.max(s, axis=-1, keepdims=True))
        alpha = jnp.exp(m_prev - m_new)
        p = jnp.exp(s - m_new)
        l_ref[h] = alpha * l_ref[h] + jnp.sum(p, axis=-1, keepdims=True)
        acc_ref[h] = alpha * acc_ref[h] + jnp.dot(p.astype(BF16), v, preferred_element_type=F32)
        m_ref[h] = m_new

    @pl.when(ki == pl.num_programs(3) - 1)
    def _():
        for h in range(group):
            o_ref[0, :, h * HEAD_DIM:(h + 1) * HEAD_DIM] = (acc_ref[h] / l_ref[h]).astype(BF16)


def _attention(q, kt, v):
    b, L, attn_w = q.shape
    group = attn_w // HEAD_DIM // N_KV_HEADS
    gw = group * HEAD_DIM
    tq, tk = min(Q_TILE, L), min(KV_TILE, L)
    grid = (b, N_KV_HEADS, L // tq, L // tk)
    return pl.pallas_call(
        functools.partial(_attn_kernel, group=group),
        grid=grid,
        in_specs=[pl.BlockSpec((1, tq, gw), lambda bi, hi, qi, ki: (bi, qi, hi)),
                  pl.BlockSpec((1, HEAD_DIM, tk), lambda bi, hi, qi, ki: (bi, hi, ki)),
                  pl.BlockSpec((1, tk, HEAD_DIM), lambda bi, hi, qi, ki: (bi, ki, hi))],
        out_specs=pl.BlockSpec((1, tq, gw), lambda bi, hi, qi, ki: (bi, qi, hi)),
        out_shape=jax.ShapeDtypeStruct((b, L, attn_w), BF16),
        scratch_shapes=[pltpu.VMEM((group, tq, 1), F32), pltpu.VMEM((group, tq, 1), F32),
                        pltpu.VMEM((group, tq, HEAD_DIM), F32)],
        compiler_params=_params("parallel", "parallel", "parallel", "arbitrary"),
        name="flash_attention",
    )(q, kt, v)


def _outproj_kernel(x_ref, ys_ref, ya_ref, ga_ref, ws_ref, wa_ref, gpost_ref, gmlp_ref, x1_ref, h_ref):
    ya = _rms(ya_ref[0].astype(F32), ga_ref[...]).astype(BF16)
    mixed = (jnp.dot(ys_ref[0], ws_ref[...], preferred_element_type=F32)
             + jnp.dot(ya, wa_ref[...], preferred_element_type=F32))
    x1 = x_ref[0] + _rms(mixed, gpost_ref[...])
    x1_ref[0] = x1
    h_ref[0] = _rms(x1, gmlp_ref[...]).astype(BF16)


def _out_projection(x, y_ssm, y_att, g_att, w_ssm, w_att, g_post, g_mlp):
    b, L, d = x.shape
    tm = min(TOKEN_TILE, L)
    tok = lambda width: pl.BlockSpec((1, tm, width), lambda bi, ti: (bi, ti, 0))
    full = lambda a: pl.BlockSpec(a.shape, lambda bi, ti: (0,) * a.ndim)
    return pl.pallas_call(
        _outproj_kernel,
        grid=(b, L // tm),
        in_specs=[tok(d), tok(y_ssm.shape[-1]), tok(y_att.shape[-1]), full(g_att), full(w_ssm), full(w_att),
                  full(g_post), full(g_mlp)],
        out_specs=[tok(d), tok(d)],
        out_shape=[jax.ShapeDtypeStruct((b, L, d), F32), jax.ShapeDtypeStruct((b, L, d), BF16)],
        compiler_params=_params("parallel", "parallel"),
        name="out_projection",
    )(x, y_ssm, y_att, g_att, w_ssm, w_att, g_post, g_mlp)


def _mlp_kernel(h_ref, x1_ref, wu_ref, wd_ref, g_ref, o_ref, acc_ref):
    j = pl.program_id(2)

    @pl.when(j == 0)
    def _():
        acc_ref[...] = jnp.zeros_like(acc_ref)

    a = jnp.dot(h_ref[0], wu_ref[...], preferred_element_type=F32)
    a = jnp.square(jnp.maximum(a, 0.0)).astype(BF16)
    acc_ref[...] += jnp.dot(a, wd_ref[...], preferred_element_type=F32)

    @pl.when(j == pl.num_programs(2) - 1)
    def _():
        o_ref[0] = x1_ref[0] + _rms(acc_ref[...], g_ref[...])


def _mlp(h, x1, w_up, w_down, gain):
    b, L, d = x1.shape
    d_ff = w_up.shape[1]
    tm, tf = min(TOKEN_TILE, L), min(FF_TILE, d_ff)
    tok = pl.BlockSpec((1, tm, d), lambda bi, ti, j: (bi, ti, 0))
    return pl.pallas_call(
        _mlp_kernel,
        grid=(b, L // tm, d_ff // tf),
        in_specs=[tok, tok,
                  pl.BlockSpec((d, tf), lambda bi, ti, j: (0, j)),
                  pl.BlockSpec((tf, d), lambda bi, ti, j: (j, 0)),
                  pl.BlockSpec(gain.shape, lambda bi, ti, j: (0, 0))],
        out_specs=tok,
        out_shape=jax.ShapeDtypeStruct((b, L, d), F32),
        scratch_shapes=[pltpu.VMEM((tm, d), F32)],
        compiler_params=_params("parallel", "parallel", "arbitrary"),
        name="mlp",
    )(h, x1, w_up, w_down, gain)


def _segment_major(a, seg_len):
    b, L, W = a.shape
    n = L // (SUBLANES * seg_len)
    return a.reshape(b, n, SUBLANES, seg_len, W).transpose(0, 1, 3, 2, 4).reshape(b, L, W)


def _token_major(a, seg_len):
    b, L, W = a.shape
    n = L // (SUBLANES * seg_len)
    return a.reshape(b, n, seg_len, SUBLANES, W).transpose(0, 1, 3, 2, 4).reshape(b, L, W)


def _prepare_weights(pre_mix_norm, w_in, ssm_a_re, ssm_a_im, ssm_log_dt, ssm_b_re, ssm_b_im,
                     ssm_c_re, ssm_c_im, ssm_d, w_glu, q_norm, k_norm, ssm_out_norm, attn_out_norm,
                     w_out, post_mix_norm, pre_mlp_norm, w_up, w_down, post_mlp_norm, seg_len):
    row = lambda g: g.reshape(1, -1).astype(F32)
    two, G, P = ssm_a_re.shape
    H = ssm_b_re.shape[-1]
    ssm_w = G * H
    lam_re, lam_im, lamseg_re, lamseg_im, bb_re, bb_im = _s5_discretise(
        ssm_a_re, ssm_a_im, ssm_log_dt, ssm_b_re, ssm_b_im, seg_len)
    gpb = GROUPS_PER_BLOCK
    nblk = G // gpb
    bmat = lambda m: _block_diag(m.reshape(two * G, P, H).transpose(0, 2, 1), gpb).reshape(
        two, nblk, gpb * H, gpb * P).astype(BF16)
    cmat = lambda m: _block_diag(m.reshape(two * G, H, P).transpose(0, 2, 1), gpb).reshape(
        two, nblk, gpb * P, gpb * H).astype(BF16)
    lmat = lambda m: jnp.broadcast_to(m.reshape(two, nblk, 1, gpb * P), (two, nblk, SUBLANES, gpb * P))
    return dict(
        pre_mix_norm=row(pre_mix_norm), w_in=w_in.astype(BF16),
        b_re=bmat(bb_re), b_im=bmat(bb_im), c_re=cmat(ssm_c_re), c_im=cmat(ssm_c_im),
        lam_re=lmat(lam_re), lam_im=lmat(lam_im), lamseg_re=lmat(lamseg_re), lamseg_im=lmat(lamseg_im),
        ssm_d=row(ssm_d), w_glu=w_glu.astype(BF16), q_norm=row(q_norm), k_norm=row(k_norm),
        ssm_out_norm=row(ssm_out_norm), attn_out_norm=row(attn_out_norm),
        w_out_ssm=w_out[:ssm_w].astype(BF16), w_out_att=w_out[ssm_w:].astype(BF16),
        post_mix_norm=row(post_mix_norm), pre_mlp_norm=row(pre_mlp_norm),
        w_up=w_up.astype(BF16), w_down=w_down.astype(BF16), post_mlp_norm=row(post_mlp_norm),
        ssm_w=ssm_w, seg_len=seg_len,
    )


def _layer(x, p):
    b, L, d = x.shape
    ssm_w = p["ssm_w"]
    kv_w = N_KV_HEADS * HEAD_DIM
    attn_w = p["w_in"].shape[1] - ssm_w - 2 * kv_w
    seg_len = p["seg_len"]
    cos_tab, sin_tab = _rope_tables(L)
    u, q, k, v = _in_projection(x, p["pre_mix_norm"], p["w_in"], cos_tab, sin_tab,
                                p["q_norm"], p["k_norm"], ssm_w, attn_w, kv_w)
    u_perm = _segment_major(u, seg_len)
    yf, yb = _s5_scan(u_perm, p["b_re"], p["b_im"], p["c_re"], p["c_im"],
                      p["lam_re"], p["lam_im"], p["lamseg_re"], p["lamseg_im"], seg_len)
    y_ssm = _token_major(_glu(yf, yb, u_perm, p["ssm_d"], p["w_glu"], p["ssm_out_norm"]), seg_len)
    y_att = _attention(q, jnp.swapaxes(k, 1, 2), v)
    x1, h = _out_projection(x, y_ssm, y_att, p["attn_out_norm"], p["w_out_ssm"], p["w_out_att"],
                            p["post_mix_norm"], p["pre_mlp_norm"])
    return _mlp(h, x1, p["w_up"], p["w_down"], p["post_mlp_norm"])


def kernel(x_prompt, x_sample, pre_mix_norm, w_in, ssm_a_re, ssm_a_im, ssm_log_dt, ssm_b_re, ssm_b_im,
           ssm_c_re, ssm_c_im, ssm_d, w_glu, q_norm, k_norm, ssm_out_norm, attn_out_norm, w_out,
           post_mix_norm, pre_mlp_norm, w_up, w_down, post_mlp_norm):
    depth = w_in.shape[0]
    y_prompt, y_sample = x_prompt, x_sample
    for l in range(depth):
        p = _prepare_weights(
            pre_mix_norm[l], w_in[l], ssm_a_re[l], ssm_a_im[l], ssm_log_dt[l], ssm_b_re[l], ssm_b_im[l],
            ssm_c_re[l], ssm_c_im[l], ssm_d[l], w_glu[l], q_norm[l], k_norm[l], ssm_out_norm[l],
            attn_out_norm[l], w_out[l], post_mix_norm[l], pre_mlp_norm[l], w_up[l], w_down[l],
            post_mlp_norm[l], SEG_LEN)
        y_prompt = _layer(y_prompt, p)
        y_sample = _layer(y_sample, p)
    return (y_prompt, y_sample)
```

```python
import functools
import math

import jax
import jax.numpy as jnp
from jax import lax
from jax.experimental import pallas as pl
from jax.experimental.pallas import tpu as pltpu

F32 = jnp.float32
BF16 = jnp.bfloat16

SSM_H = 16
SSM_P = 64
HEAD_DIM = 128
N_KV_HEADS = 2
GRID_W = 64
ROPE_THETA = 10000.0
EPS = 1e-6

LANES = 128
SUBLANES = 8
MXU_DEPTH = 256
VMEM_LIMIT_BYTES = 56 * 1024 * 1024

TOKEN_TILE = 512
FF_TILE = 512
GROUPS_PER_BLOCK = LANES // SSM_H
SEG_LEN = 128
SCAN_CHUNK = SUBLANES * SEG_LEN
Q_TILE = 1024
KV_TILE = 1024
ATTN_ROW_BLOCK = 256


def _params(*semantics):
    return pltpu.CompilerParams(dimension_semantics=semantics, vmem_limit_bytes=VMEM_LIMIT_BYTES)


def _rms(x, g):
    return x * lax.rsqrt(jnp.mean(x * x, axis=-1, keepdims=True) + EPS) * g


def _inproj_kernel(x_ref, g_ref, w_ref, cos_ref, sin_ref, qg_ref, kg_ref,
                   u_ref, q_ref, k_ref, v_ref, *, ssm_w, attn_w, kv_w):
    h = _rms(x_ref[0], g_ref[...]).astype(BF16)

    def proj(lo, width):
        return jnp.dot(h, w_ref[:, lo:lo + width], preferred_element_type=F32)

    u_ref[0] = proj(0, ssm_w).astype(BF16)
    v_ref[0] = proj(ssm_w + attn_w + kv_w, kv_w).astype(BF16)

    cos = cos_ref[...]
    sin = sin_ref[...]
    lane = lax.broadcasted_iota(jnp.int32, cos.shape, 1)
    first_half = (lane & (HEAD_DIM // 4)) == 0

    def norm_rope(xh, gain):
        n = _rms(xh, gain)
        partner = jnp.where(first_half,
                            pltpu.roll(n, HEAD_DIM - HEAD_DIM // 4, 1),
                            pltpu.roll(n, HEAD_DIM // 4, 1))
        return n * cos + partner * sin

    scale = math.log2(math.e) / math.sqrt(HEAD_DIM)
    for hh in range(attn_w // HEAD_DIM):
        qh = proj(ssm_w + hh * HEAD_DIM, HEAD_DIM)
        q_ref[0, hh] = (norm_rope(qh, qg_ref[...]) * scale).astype(BF16)
    for hh in range(kv_w // HEAD_DIM):
        kh = proj(ssm_w + attn_w + hh * HEAD_DIM, HEAD_DIM)
        k_ref[0, :, hh * HEAD_DIM:(hh + 1) * HEAD_DIM] = norm_rope(kh, kg_ref[...]).astype(BF16)


def _in_projection(x, gain, w_in, cos_tab, sin_tab, q_gain, k_gain, ssm_w, attn_w, kv_w):
    b, L, d = x.shape
    tm = min(TOKEN_TILE, L)
    heads = attn_w // HEAD_DIM
    grid = (b, L // tm)
    tok = lambda width: pl.BlockSpec((1, tm, width), lambda bi, ti: (bi, ti, 0))
    full = lambda a: pl.BlockSpec(a.shape, lambda bi, ti: (0,) * a.ndim)
    tab = pl.BlockSpec((tm, HEAD_DIM), lambda bi, ti: (ti, 0))
    return pl.pallas_call(
        functools.partial(_inproj_kernel, ssm_w=ssm_w, attn_w=attn_w, kv_w=kv_w),
        grid=grid,
        in_specs=[tok(d), full(gain), full(w_in), tab, tab, full(q_gain), full(k_gain)],
        out_specs=[tok(ssm_w), pl.BlockSpec((1, heads, tm, HEAD_DIM), lambda bi, ti: (bi, 0, ti, 0)),
                   tok(kv_w), tok(kv_w)],
        out_shape=[jax.ShapeDtypeStruct((b, L, ssm_w), BF16),
                   jax.ShapeDtypeStruct((b, heads, L, HEAD_DIM), BF16),
                   jax.ShapeDtypeStruct((b, L, kv_w), BF16),
                   jax.ShapeDtypeStruct((b, L, kv_w), BF16)],
        compiler_params=_params("parallel", "parallel"),
        name="in_projection",
    )(x, gain, w_in, cos_tab, sin_tab, q_gain, k_gain)


def _rope_tables(L):
    axis_dim = HEAD_DIM // 2
    rows = L // GRID_W
    inv_freq = ROPE_THETA ** (-jnp.arange(0, axis_dim, 2, dtype=F32) / axis_dim)
    ang_r = jnp.arange(rows, dtype=F32)[:, None] * inv_freq[None, :]
    ang_c = jnp.arange(GRID_W, dtype=F32)[:, None] * inv_freq[None, :]
    rep = lambda t: jnp.repeat(t, GRID_W, axis=0)
    til = lambda t: jnp.tile(t, (rows, 1))
    cr, sr, cc, sc = rep(jnp.cos(ang_r)), rep(jnp.sin(ang_r)), til(jnp.cos(ang_c)), til(jnp.sin(ang_c))
    return (jnp.concatenate([cr, cr, cc, cc], axis=1),
            jnp.concatenate([-sr, sr, -sc, sc], axis=1))


def _zoh(lr, li, dt):
    mag = jnp.exp(lr * dt)
    lb_re, lb_im = mag * jnp.cos(li * dt), mag * jnp.sin(li * dt)
    den = lr * lr + li * li
    nr, ni = lb_re - 1.0, lb_im
    return lb_re, lb_im, (nr * lr + ni * li) / den, (ni * lr - nr * li) / den


def _s5_prep_kernel(lr_ref, li_ref, ldt_ref, lrx_ref, lix_ref, ldtx_ref, br_ref, bi_ref,
                    lam_re_ref, lam_im_ref, lamseg_re_ref, lamseg_im_ref, bb_re_ref, bb_im_ref, *, seg_len):
    lb_re, lb_im, _, _ = _zoh(lr_ref[...], li_ref[...], jnp.exp(ldt_ref[...]))
    lam_re_ref[...] = lb_re
    lam_im_ref[...] = lb_im
    pr, pi = lb_re, lb_im
    for _ in range(int(math.log2(seg_len))):
        pr, pi = pr * pr - pi * pi, 2.0 * pr * pi
    lamseg_re_ref[...] = pr
    lamseg_im_ref[...] = pi
    _, _, w_re, w_im = _zoh(lrx_ref[...], lix_ref[...], jnp.exp(ldtx_ref[...]))
    br, bi = br_ref[...], bi_ref[...]
    bb_re_ref[...] = w_re * br - w_im * bi
    bb_im_ref[...] = w_re * bi + w_im * br


def _s5_discretise(a_re, a_im, log_dt, b_re, b_im, seg_len):
    two, G, P = a_re.shape
    H = b_re.shape[-1]
    rows = two * G
    small = lambda a: a.reshape(rows, P)
    wide = lambda a: jnp.repeat(a.reshape(rows, P), H, axis=1)
    ldt = jnp.broadcast_to(log_dt.reshape(rows, 1), (rows, P))
    args = (small(a_re), small(a_im), ldt, wide(a_re), wide(a_im), jnp.repeat(ldt, H, axis=1),
            b_re.reshape(rows, P * H), b_im.reshape(rows, P * H))
    sds = lambda w: jax.ShapeDtypeStruct((rows, w), F32)
    return pl.pallas_call(
        functools.partial(_s5_prep_kernel, seg_len=seg_len),
        out_shape=[sds(P)] * 4 + [sds(P * H)] * 2,
        name="s5_discretise",
    )(*args)


def _block_diag(m, per_block):
    n, r, c = m.shape
    nb = n // per_block
    eye = jnp.eye(per_block, dtype=m.dtype)
    out = m.reshape(nb, per_block, r, 1, c) * eye[None, :, None, :, None]
    return out.reshape(nb, per_block * r, per_block * c)


def _s5_scan_kernel(uf_ref, ub_ref, bre_ref, bim_ref, cre_ref, cim_ref,
                    lre_ref, lim_ref, sre_ref, sim_ref,
                    yf_ref, yb_ref,
                    xr_ref, xi_ref, car_ref, *, seg_len, col_tile):
    @pl.when(pl.program_id(2) == 0)
    def _():
        car_ref[...] = jnp.zeros_like(car_ref)

    n_tiles = xr_ref.shape[1]
    u_refs, y_refs = (uf_ref, ub_ref), (yf_ref, yb_ref)
    streams = [(d, n) for d in range(2) for n in range(n_tiles)]

    def project(d, n):
        cols = pl.ds(n * col_tile, col_tile)
        u = u_refs[d][0]
        xr_ref[d, n] = jnp.dot(u, bre_ref[d, 0, :, cols], preferred_element_type=F32)
        xi_ref[d, n] = jnp.dot(u, bim_ref[d, 0, :, cols], preferred_element_type=F32)

    def recur(d, n):
        cols = pl.ds(n * col_tile, col_tile)
        xr, xi = xr_ref.at[d, n], xi_ref.at[d, n]
        steps = list(range(seg_len)) if d == 0 else list(reversed(range(seg_len)))
        segments = list(range(SUBLANES)) if d == 0 else list(reversed(range(SUBLANES)))
        lr, li = lre_ref[d, 0, :, cols], lim_ref[d, 0, :, cols]

        er = ei = jnp.zeros_like(lr)
        for t in steps:
            rows = pl.ds(t * SUBLANES, SUBLANES)
            er, ei = lr * er - li * ei + xr[rows, :], lr * ei + li * er + xi[rows, :]

        sr, si = sre_ref[d, 0, 0:1, cols], sim_ref[d, 0, 0:1, cols]
        cr, ci = car_ref[d, n, 0:1], car_ref[d, n, 1:2]
        rows_r, rows_i = [None] * SUBLANES, [None] * SUBLANES
        for j in segments:
            rows_r[j], rows_i[j] = cr, ci
            cr, ci = sr * cr - si * ci + er[j:j + 1], sr * ci + si * cr + ei[j:j + 1]
        car_ref[d, n, 0:1] = cr
        car_ref[d, n, 1:2] = ci
        zr, zi = jnp.concatenate(rows_r, axis=0), jnp.concatenate(rows_i, axis=0)

        for t in steps:
            rows = pl.ds(t * SUBLANES, SUBLANES)
            zr, zi = lr * zr - li * zi + xr[rows, :], lr * zi + li * zr + xi[rows, :]
            xr[rows, :] = zr
            xi[rows, :] = zi

    def readout(d, n):
        cols = pl.ds(n * col_tile, col_tile)
        return (jnp.dot(xr_ref[d, n].astype(BF16), cre_ref[d, 0, cols, :], preferred_element_type=F32)
                - jnp.dot(xi_ref[d, n].astype(BF16), cim_ref[d, 0, cols, :], preferred_element_type=F32))

    y = [None, None]
    project(*streams[0])
    for k, (d, n) in enumerate(streams):
        if k + 1 < len(streams):
            project(*streams[k + 1])
        recur(d, n)
        part = readout(d, n)
        y[d] = part if y[d] is None else y[d] + part
    for d in range(2):
        y_refs[d][0] = y[d].astype(BF16)


def _s5_scan(u_perm, b_re, b_im, c_re, c_im, lam_re, lam_im, lamseg_re, lamseg_im, seg_len):
    b, L, W = u_perm.shape
    chunk = SUBLANES * seg_len
    nblk, nchunk = W // LANES, L // chunk
    n_state = b_re.shape[-1]
    grid = (b, nblk, nchunk)
    fwd = pl.BlockSpec((1, chunk, LANES), lambda bi, gi, ci: (bi, ci, gi))
    bwd = pl.BlockSpec((1, chunk, LANES), lambda bi, gi, ci: (bi, nchunk - 1 - ci, gi))
    wspec = lambda a: pl.BlockSpec((2, 1) + a.shape[2:], lambda bi, gi, ci: (0, gi, 0, 0))
    out = jax.ShapeDtypeStruct((b, L, W), BF16)
    col_tile = min(MXU_DEPTH, n_state)
    n_tiles = n_state // col_tile
    state = pltpu.VMEM((2, n_tiles, chunk, col_tile), F32)
    return pl.pallas_call(
        functools.partial(_s5_scan_kernel, seg_len=seg_len, col_tile=col_tile),
        grid=grid,
        in_specs=[fwd, bwd, wspec(b_re), wspec(b_im), wspec(c_re), wspec(c_im),
                  wspec(lam_re), wspec(lam_im), wspec(lamseg_re), wspec(lamseg_im)],
        out_specs=[fwd, bwd],
        out_shape=[out, out],
        scratch_shapes=[state, state, pltpu.VMEM((2, n_tiles, SUBLANES, col_tile), F32)],
        compiler_params=_params("parallel", "parallel", "arbitrary"),
        name="s5_scan",
    )(u_perm, u_perm, b_re, b_im, c_re, c_im, lam_re, lam_im, lamseg_re, lamseg_im)


def _glu_kernel(yf_ref, yb_ref, u_ref, d_ref, w_ref, g_ref, o_ref):
    y = yf_ref[0].astype(F32) + yb_ref[0].astype(F32) + d_ref[...] * u_ref[0].astype(F32)
    z = jax.nn.gelu(y)
    gate = jnp.dot(z.astype(BF16), w_ref[...], preferred_element_type=F32)
    o_ref[0] = _rms(z * jax.nn.sigmoid(gate), g_ref[...]).astype(BF16)


def _glu(yf, yb, u, d, w_glu, gain):
    b, L, W = u.shape
    tm = min(TOKEN_TILE, L)
    tok = pl.BlockSpec((1, tm, W), lambda bi, ti: (bi, ti, 0))
    full = lambda a: pl.BlockSpec(a.shape, lambda bi, ti: (0,) * a.ndim)
    return pl.pallas_call(
        _glu_kernel,
        grid=(b, L // tm),
        in_specs=[tok, tok, tok, full(d), full(w_glu), full(gain)],
        out_specs=tok,
        out_shape=jax.ShapeDtypeStruct((b, L, W), BF16),
        compiler_params=_params("parallel", "parallel"),
        name="ssm_glu",
    )(yf, yb, u, d, w_glu, gain)


def _attn_kernel(q_ref, kt_ref, v_ref, o_ref, m_ref, acc_ref, s_ref, mnew_ref, alpha_ref, vext_ref, *,
                 group, row_block, kv_chunk):
    ki = pl.program_id(3)
    tq = q_ref.shape[2]
    tk = kt_ref.shape[2]
    per_head = tq // row_block
    nblk = group * per_head

    @pl.when(ki == 0)
    def _():
        m_ref[...] = jnp.full_like(m_ref, -jnp.inf)
        acc_ref[...] = jnp.zeros_like(acc_ref)

    vext_ref[:, :HEAD_DIM] = v_ref[0]
    vext_ref[:, HEAD_DIM:] = jnp.ones((vext_ref.shape[0], HEAD_DIM), BF16)

    def where(i):
        return i // per_head, pl.ds((i % per_head) * row_block, row_block)

    def scores(i):
        h, rows = where(i)
        s = jnp.dot(q_ref[0, h, rows, :], kt_ref[0], preferred_element_type=F32)
        s_ref[i % 2] = s
        m_prev = m_ref[h, rows, :]
        m_new = jnp.maximum(m_prev, jnp.max(s, axis=-1, keepdims=True))
        m_ref[h, rows, :] = m_new
        mnew_ref[i % 2] = m_new
        alpha_ref[i % 2] = jnp.exp2(m_prev - m_new)

    def softmax_pv(i):
        h, rows = where(i)
        slot = i % 2
        m_new = mnew_ref[slot]
        pv = None
        for c in range(tk // kv_chunk):
            cols = pl.ds(c * kv_chunk, kv_chunk)
            p = jnp.exp2(s_ref[slot, :, cols] - m_new).astype(BF16)
            part = jnp.dot(p, vext_ref[cols, :], preferred_element_type=F32)
            pv = part if pv is None else pv + part
        acc_ref[h, rows, :] = alpha_ref[slot] * acc_ref[h, rows, :] + pv

    scores(0)
    for i in range(nblk - 1):
        scores(i + 1)
        softmax_pv(i)
    softmax_pv(nblk - 1)

    @pl.when(ki == pl.num_programs(3) - 1)
    def _():
        for h in range(group):
            o_ref[0, h] = (acc_ref[h, :, :HEAD_DIM] / acc_ref[h, :, HEAD_DIM:]).astype(BF16)


def _attention(q, kt, v):
    b, heads, L, _ = q.shape
    group = heads // N_KV_HEADS
    tq, tk = min(Q_TILE, L), min(KV_TILE, L)
    row_block = min(ATTN_ROW_BLOCK, tq)
    grid = (b, N_KV_HEADS, L // tq, L // tk)
    qspec = pl.BlockSpec((1, group, tq, HEAD_DIM), lambda bi, hi, qi, ki: (bi, hi, qi, 0))
    return pl.pallas_call(
        functools.partial(_attn_kernel, group=group, row_block=row_block, kv_chunk=min(MXU_DEPTH, tk)),
        grid=grid,
        in_specs=[qspec,
                  pl.BlockSpec((1, HEAD_DIM, tk), lambda bi, hi, qi, ki: (bi, hi, ki)),
                  pl.BlockSpec((1, tk, HEAD_DIM), lambda bi, hi, qi, ki: (bi, ki, hi))],
        out_specs=qspec,
        out_shape=jax.ShapeDtypeStruct((b, heads, L, HEAD_DIM), BF16),
        scratch_shapes=[pltpu.VMEM((group, tq, 1), F32),
                        pltpu.VMEM((group, tq, 2 * HEAD_DIM), F32),
                        pltpu.VMEM((2, row_block, tk), F32),
                        pltpu.VMEM((2, row_block, 1), F32),
                        pltpu.VMEM((2, row_block, 1), F32),
                        pltpu.VMEM((tk, 2 * HEAD_DIM), BF16)],
        compiler_params=_params("parallel", "parallel", "parallel", "arbitrary"),
        name="flash_attention",
    )(q, kt, v)


def _outproj_kernel(x_ref, ys_ref, ya_ref, ga_ref, ws_ref, wa_ref, gpost_ref, gmlp_ref, x1_ref, h_ref):
    ya = jnp.concatenate([ya_ref[0, hh] for hh in range(ya_ref.shape[1])], axis=-1)
    ya = _rms(ya.astype(F32), ga_ref[...]).astype(BF16)
    mixed = (jnp.dot(ys_ref[0], ws_ref[...], preferred_element_type=F32)
             + jnp.dot(ya, wa_ref[...], preferred_element_type=F32))
    x1 = x_ref[0] + _rms(mixed, gpost_ref[...])
    x1_ref[0] = x1
    h_ref[0] = _rms(x1, gmlp_ref[...]).astype(BF16)


def _out_projection(x, y_ssm, y_att, g_att, w_ssm, w_att, g_post, g_mlp):
    b, L, d = x.shape
    tm = min(TOKEN_TILE, L)
    tok = lambda width: pl.BlockSpec((1, tm, width), lambda bi, ti: (bi, ti, 0))
    full = lambda a: pl.BlockSpec(a.shape, lambda bi, ti: (0,) * a.ndim)
    return pl.pallas_call(
        _outproj_kernel,
        grid=(b, L // tm),
        in_specs=[tok(d), tok(y_ssm.shape[-1]),
                  pl.BlockSpec((1, y_att.shape[1], tm, HEAD_DIM), lambda bi, ti: (bi, 0, ti, 0)),
                  full(g_att), full(w_ssm), full(w_att),
                  full(g_post), full(g_mlp)],
        out_specs=[tok(d), tok(d)],
        out_shape=[jax.ShapeDtypeStruct((b, L, d), F32), jax.ShapeDtypeStruct((b, L, d), BF16)],
        compiler_params=_params("parallel", "parallel"),
        name="out_projection",
    )(x, y_ssm, y_att, g_att, w_ssm, w_att, g_post, g_mlp)


def _mlp_kernel(h_ref, x1_ref, wu_ref, wd_ref, g_ref, o_ref, acc_ref):
    j = pl.program_id(2)

    @pl.when(j == 0)
    def _():
        acc_ref[...] = jnp.zeros_like(acc_ref)

    a = jnp.dot(h_ref[0], wu_ref[...], preferred_element_type=F32)
    a = jnp.square(jnp.maximum(a, 0.0)).astype(BF16)
    acc_ref[...] += jnp.dot(a, wd_ref[...], preferred_element_type=F32)

    @pl.when(j == pl.num_programs(2) - 1)
    def _():
        o_ref[0] = x1_ref[0] + _rms(acc_ref[...], g_ref[...])


def _mlp(h, x1, w_up, w_down, gain):
    b, L, d = x1.shape
    d_ff = w_up.shape[1]
    tm, tf = min(TOKEN_TILE, L), min(FF_TILE, d_ff)
    tok = pl.BlockSpec((1, tm, d), lambda bi, ti, j: (bi, ti, 0))
    return pl.pallas_call(
        _mlp_kernel,
        grid=(b, L // tm, d_ff // tf),
        in_specs=[tok, tok,
                  pl.BlockSpec((d, tf), lambda bi, ti, j: (0, j)),
                  pl.BlockSpec((tf, d), lambda bi, ti, j: (j, 0)),
                  pl.BlockSpec(gain.shape, lambda bi, ti, j: (0, 0))],
        out_specs=tok,
        out_shape=jax.ShapeDtypeStruct((b, L, d), F32),
        scratch_shapes=[pltpu.VMEM((tm, d), F32)],
        compiler_params=_params("parallel", "parallel", "arbitrary"),
        name="mlp",
    )(h, x1, w_up, w_down, gain)


def _segment_major(a, seg_len):
    b, L, W = a.shape
    n = L // (SUBLANES * seg_len)
    return a.reshape(b, n, SUBLANES, seg_len, W).transpose(0, 1, 3, 2, 4).reshape(b, L, W)


def _token_major(a, seg_len):
    b, L, W = a.shape
    n = L // (SUBLANES * seg_len)
    return a.reshape(b, n, seg_len, SUBLANES, W).transpose(0, 1, 3, 2, 4).reshape(b, L, W)


def _prepare_weights(pre_mix_norm, w_in, ssm_a_re, ssm_a_im, ssm_log_dt, ssm_b_re, ssm_b_im,
                     ssm_c_re, ssm_c_im, ssm_d, w_glu, q_norm, k_norm, ssm_out_norm, attn_out_norm,
                     w_out, post_mix_norm, pre_mlp_norm, w_up, w_down, post_mlp_norm, seg_len):
    row = lambda g: g.reshape(1, -1).astype(F32)
    two, G, P = ssm_a_re.shape
    H = ssm_b_re.shape[-1]
    ssm_w = G * H
    lam_re, lam_im, lamseg_re, lamseg_im, bb_re, bb_im = _s5_discretise(
        ssm_a_re, ssm_a_im, ssm_log_dt, ssm_b_re, ssm_b_im, seg_len)
    gpb = GROUPS_PER_BLOCK
    nblk = G // gpb
    bmat = lambda m: _block_diag(m.reshape(two * G, P, H).transpose(0, 2, 1), gpb).reshape(
        two, nblk, gpb * H, gpb * P).astype(BF16)
    cmat = lambda m: _block_diag(m.reshape(two * G, H, P).transpose(0, 2, 1), gpb).reshape(
        two, nblk, gpb * P, gpb * H).astype(BF16)
    lmat = lambda m: jnp.broadcast_to(m.reshape(two, nblk, 1, gpb * P), (two, nblk, SUBLANES, gpb * P))
    return dict(
        pre_mix_norm=row(pre_mix_norm), w_in=w_in.astype(BF16),
        b_re=bmat(bb_re), b_im=bmat(bb_im), c_re=cmat(ssm_c_re), c_im=cmat(ssm_c_im),
        lam_re=lmat(lam_re), lam_im=lmat(lam_im), lamseg_re=lmat(lamseg_re), lamseg_im=lmat(lamseg_im),
        ssm_d=row(ssm_d), w_glu=w_glu.astype(BF16), q_norm=row(q_norm), k_norm=row(k_norm),
        ssm_out_norm=row(ssm_out_norm), attn_out_norm=row(attn_out_norm),
        w_out_ssm=w_out[:ssm_w].astype(BF16), w_out_att=w_out[ssm_w:].astype(BF16),
        post_mix_norm=row(post_mix_norm), pre_mlp_norm=row(pre_mlp_norm),
        w_up=w_up.astype(BF16), w_down=w_down.astype(BF16), post_mlp_norm=row(post_mlp_norm),
        ssm_w=ssm_w, seg_len=seg_len,
    )


def _layer(x, p):
    b, L, d = x.shape
    ssm_w = p["ssm_w"]
    kv_w = N_KV_HEADS * HEAD_DIM
    attn_w = p["w_in"].shape[1] - ssm_w - 2 * kv_w
    seg_len = p["seg_len"]
    cos_tab, sin_tab = _rope_tables(L)
    u, q, k, v = _in_projection(x, p["pre_mix_norm"], p["w_in"], cos_tab, sin_tab,
                                p["q_norm"], p["k_norm"], ssm_w, attn_w, kv_w)
    u_perm = _segment_major(u, seg_len)
    yf, yb = _s5_scan(u_perm, p["b_re"], p["b_im"], p["c_re"], p["c_im"],
                      p["lam_re"], p["lam_im"], p["lamseg_re"], p["lamseg_im"], seg_len)
    y_ssm = _token_major(_glu(yf, yb, u_perm, p["ssm_d"], p["w_glu"], p["ssm_out_norm"]), seg_len)
    y_att = _attention(q, jnp.swapaxes(k, 1, 2), v)
    x1, h = _out_projection(x, y_ssm, y_att, p["attn_out_norm"], p["w_out_ssm"], p["w_out_att"],
                            p["post_mix_norm"], p["pre_mlp_norm"])
    return _mlp(h, x1, p["w_up"], p["w_down"], p["post_mlp_norm"])


def kernel(x_prompt, x_sample, pre_mix_norm, w_in, ssm_a_re, ssm_a_im, ssm_log_dt, ssm_b_re, ssm_b_im,
           ssm_c_re, ssm_c_im, ssm_d, w_glu, q_norm, k_norm, ssm_out_norm, attn_out_norm, w_out,
           post_mix_norm, pre_mlp_norm, w_up, w_down, post_mlp_norm):
    depth = w_in.shape[0]
    y_prompt, y_sample = x_prompt, x_sample
    for l in range(depth):
        p = _prepare_weights(
            pre_mix_norm[l], w_in[l], ssm_a_re[l], ssm_a_im[l], ssm_log_dt[l], ssm_b_re[l], ssm_b_im[l],
            ssm_c_re[l], ssm_c_im[l], ssm_d[l], w_glu[l], q_norm[l], k_norm[l], ssm_out_norm[l],
            attn_out_norm[l], w_out[l], post_mix_norm[l], pre_mlp_norm[l], w_up[l], w_down[l],
            post_mlp_norm[l], SEG_LEN)
        y_prompt = _layer(y_prompt, p)
        y_sample = _layer(y_sample, p)
    return (y_prompt, y_sample)
```

```python
import functools
import math

import jax
import jax.numpy as jnp
from jax import lax
from jax.experimental import pallas as pl
from jax.experimental.pallas import tpu as pltpu

F32 = jnp.float32
BF16 = jnp.bfloat16

SSM_H = 16
SSM_P = 64
HEAD_DIM = 128
N_KV_HEADS = 2
GRID_W = 64
ROPE_THETA = 10000.0
EPS = 1e-6

LANES = 128
SUBLANES = 8
MXU_DEPTH = 256
VMEM_LIMIT_BYTES = 56 * 1024 * 1024

TOKEN_TILE = 512
FF_TILE = 1024
GROUPS_PER_BLOCK = LANES // SSM_H
SEG_LEN = 128
SCAN_CHUNK = SUBLANES * SEG_LEN
Q_TILE = 1024
KV_TILE = 4096
ATTN_ROW_BLOCK = 256


def _params(*semantics):
    return pltpu.CompilerParams(dimension_semantics=semantics, vmem_limit_bytes=VMEM_LIMIT_BYTES)


def _rms(x, g):
    return x * lax.rsqrt(jnp.mean(x * x, axis=-1, keepdims=True) + EPS) * g


def _inproj_kernel(x_ref, g_ref, w_ref, cos_ref, sin_ref, qg_ref, kg_ref,
                   u_ref, q_ref, k_ref, v_ref, *, ssm_w, attn_w, kv_w):
    h = _rms(x_ref[0], g_ref[...]).astype(BF16)

    def proj(lo, width):
        return jnp.dot(h, w_ref[:, lo:lo + width], preferred_element_type=F32)

    cos = cos_ref[...]
    sin = sin_ref[...]
    lane = lax.broadcasted_iota(jnp.int32, cos.shape, 1)
    first_half = (lane & (HEAD_DIM // 4)) == 0

    def norm_rope(xh, gain):
        n = _rms(xh, gain)
        partner = jnp.where(first_half,
                            pltpu.roll(n, HEAD_DIM - HEAD_DIM // 4, 1),
                            pltpu.roll(n, HEAD_DIM // 4, 1))
        return n * cos + partner * sin

    scale = math.log2(math.e) / math.sqrt(HEAD_DIM)
    per_dot = MXU_DEPTH // HEAD_DIM
    for h0 in range(0, attn_w // HEAD_DIM, per_dot):
        qq = proj(ssm_w + h0 * HEAD_DIM, MXU_DEPTH)
        for j in range(per_dot):
            qh = qq[:, j * HEAD_DIM:(j + 1) * HEAD_DIM]
            q_ref[0, h0 + j] = (norm_rope(qh, qg_ref[...]) * scale).astype(BF16)
    for h0 in range(0, kv_w // HEAD_DIM, per_dot):
        kk = proj(ssm_w + attn_w + h0 * HEAD_DIM, MXU_DEPTH)
        for j in range(per_dot):
            kh = kk[:, j * HEAD_DIM:(j + 1) * HEAD_DIM]
            k_ref[0, :, (h0 + j) * HEAD_DIM:(h0 + j + 1) * HEAD_DIM] = norm_rope(kh, kg_ref[...]).astype(BF16)
    for c0 in range(0, ssm_w, MXU_DEPTH):
        u_ref[0, :, c0:c0 + MXU_DEPTH] = proj(c0, MXU_DEPTH).astype(BF16)
    v_ref[0] = proj(ssm_w + attn_w + kv_w, kv_w).astype(BF16)


def _in_projection(x, gain, w_in, cos_tab, sin_tab, q_gain, k_gain, ssm_w, attn_w, kv_w):
    b, L, d = x.shape
    tm = min(TOKEN_TILE, L)
    heads = attn_w // HEAD_DIM
    grid = (b, L // tm)
    tok = lambda width: pl.BlockSpec((1, tm, width), lambda bi, ti: (bi, ti, 0))
    full = lambda a: pl.BlockSpec(a.shape, lambda bi, ti: (0,) * a.ndim)
    tab = pl.BlockSpec((tm, HEAD_DIM), lambda bi, ti: (ti, 0))
    return pl.pallas_call(
        functools.partial(_inproj_kernel, ssm_w=ssm_w, attn_w=attn_w, kv_w=kv_w),
        grid=grid,
        in_specs=[tok(d), full(gain), full(w_in), tab, tab, full(q_gain), full(k_gain)],
        out_specs=[tok(ssm_w), pl.BlockSpec((1, heads, tm, HEAD_DIM), lambda bi, ti: (bi, 0, ti, 0)),
                   tok(kv_w), tok(kv_w)],
        out_shape=[jax.ShapeDtypeStruct((b, L, ssm_w), BF16),
                   jax.ShapeDtypeStruct((b, heads, L, HEAD_DIM), BF16),
                   jax.ShapeDtypeStruct((b, L, kv_w), BF16),
                   jax.ShapeDtypeStruct((b, L, kv_w), BF16)],
        compiler_params=_params("parallel", "parallel"),
        name="in_projection",
    )(x, gain, w_in, cos_tab, sin_tab, q_gain, k_gain)


def _rope_tables(L):
    axis_dim = HEAD_DIM // 2
    rows = L // GRID_W
    inv_freq = ROPE_THETA ** (-jnp.arange(0, axis_dim, 2, dtype=F32) / axis_dim)
    ang_r = jnp.arange(rows, dtype=F32)[:, None] * inv_freq[None, :]
    ang_c = jnp.arange(GRID_W, dtype=F32)[:, None] * inv_freq[None, :]
    rep = lambda t: jnp.repeat(t, GRID_W, axis=0)
    til = lambda t: jnp.tile(t, (rows, 1))
    cr, sr, cc, sc = rep(jnp.cos(ang_r)), rep(jnp.sin(ang_r)), til(jnp.cos(ang_c)), til(jnp.sin(ang_c))
    return (jnp.concatenate([cr, cr, cc, cc], axis=1),
            jnp.concatenate([-sr, sr, -sc, sc], axis=1))


def _zoh(lr, li, dt):
    mag = jnp.exp(lr * dt)
    lb_re, lb_im = mag * jnp.cos(li * dt), mag * jnp.sin(li * dt)
    den = lr * lr + li * li
    nr, ni = lb_re - 1.0, lb_im
    return lb_re, lb_im, (nr * lr + ni * li) / den, (ni * lr - nr * li) / den


def _s5_prep_kernel(lr_ref, li_ref, ldt_ref, lrx_ref, lix_ref, ldtx_ref, br_ref, bi_ref,
                    lam_re_ref, lam_im_ref, lamseg_re_ref, lamseg_im_ref, bb_re_ref, bb_im_ref, *, seg_len):
    lb_re, lb_im, _, _ = _zoh(lr_ref[...], li_ref[...], jnp.exp(ldt_ref[...]))
    lam_re_ref[...] = lb_re
    lam_im_ref[...] = lb_im
    pr, pi = lb_re, lb_im
    for _ in range(int(math.log2(seg_len))):
        pr, pi = pr * pr - pi * pi, 2.0 * pr * pi
    lamseg_re_ref[...] = pr
    lamseg_im_ref[...] = pi
    _, _, w_re, w_im = _zoh(lrx_ref[...], lix_ref[...], jnp.exp(ldtx_ref[...]))
    br, bi = br_ref[...], bi_ref[...]
    bb_re_ref[...] = w_re * br - w_im * bi
    bb_im_ref[...] = w_re * bi + w_im * br


def _s5_discretise(a_re, a_im, log_dt, b_re, b_im, seg_len):
    two, G, P = a_re.shape
    H = b_re.shape[-1]
    rows = two * G
    small = lambda a: a.reshape(rows, P)
    wide = lambda a: jnp.repeat(a.reshape(rows, P), H, axis=1)
    ldt = jnp.broadcast_to(log_dt.reshape(rows, 1), (rows, P))
    args = (small(a_re), small(a_im), ldt, wide(a_re), wide(a_im), jnp.repeat(ldt, H, axis=1),
            b_re.reshape(rows, P * H), b_im.reshape(rows, P * H))
    sds = lambda w: jax.ShapeDtypeStruct((rows, w), F32)
    return pl.pallas_call(
        functools.partial(_s5_prep_kernel, seg_len=seg_len),
        out_shape=[sds(P)] * 4 + [sds(P * H)] * 2,
        name="s5_discretise",
    )(*args)


def _block_diag(m, per_block):
    n, r, c = m.shape
    nb = n // per_block
    eye = jnp.eye(per_block, dtype=m.dtype)
    out = m.reshape(nb, per_block, r, 1, c) * eye[None, :, None, :, None]
    return out.reshape(nb, per_block * r, per_block * c)


def _s5_scan_kernel(uf_ref, ub_ref, bre_ref, bim_ref, cre_ref, cim_ref,
                    lre_ref, lim_ref, sre_ref, sim_ref,
                    yf_ref, yb_ref,
                    xr_ref, xi_ref, car_ref, *, seg_len, col_tile):
    @pl.when(pl.program_id(2) == 0)
    def _():
        car_ref[...] = jnp.zeros_like(car_ref)

    n_tiles = xr_ref.shape[1]
    u_refs, y_refs = (uf_ref, ub_ref), (yf_ref, yb_ref)
    streams = [(d, n) for d in range(2) for n in range(n_tiles)]

    def project(d, n):
        cols = pl.ds(n * col_tile, col_tile)
        u = u_refs[d][0]
        xr_ref[d, n] = jnp.dot(u, bre_ref[d, 0, :, cols], preferred_element_type=F32)
        xi_ref[d, n] = jnp.dot(u, bim_ref[d, 0, :, cols], preferred_element_type=F32)

    def recur(d, n):
        cols = pl.ds(n * col_tile, col_tile)
        xr, xi = xr_ref.at[d, n], xi_ref.at[d, n]
        steps = list(range(seg_len)) if d == 0 else list(reversed(range(seg_len)))
        segments = list(range(SUBLANES)) if d == 0 else list(reversed(range(SUBLANES)))
        lr, li = lre_ref[d, 0, :, cols], lim_ref[d, 0, :, cols]

        er = ei = jnp.zeros_like(lr)
        for t in steps:
            rows = pl.ds(t * SUBLANES, SUBLANES)
            er, ei = lr * er - li * ei + xr[rows, :], lr * ei + li * er + xi[rows, :]

        sr, si = sre_ref[d, 0, 0:1, cols], sim_ref[d, 0, 0:1, cols]
        cr, ci = car_ref[d, n, 0:1], car_ref[d, n, 1:2]
        rows_r, rows_i = [None] * SUBLANES, [None] * SUBLANES
        for j in segments:
            rows_r[j], rows_i[j] = cr, ci
            cr, ci = sr * cr - si * ci + er[j:j + 1], sr * ci + si * cr + ei[j:j + 1]
        car_ref[d, n, 0:1] = cr
        car_ref[d, n, 1:2] = ci
        zr, zi = jnp.concatenate(rows_r, axis=0), jnp.concatenate(rows_i, axis=0)

        for t in steps:
            rows = pl.ds(t * SUBLANES, SUBLANES)
            zr, zi = lr * zr - li * zi + xr[rows, :], lr * zi + li * zr + xi[rows, :]
            xr[rows, :] = zr
            xi[rows, :] = zi

    def readout(d, n):
        cols = pl.ds(n * col_tile, col_tile)
        return (jnp.dot(xr_ref[d, n].astype(BF16), cre_ref[d, 0, cols, :], preferred_element_type=F32)
                - jnp.dot(xi_ref[d, n].astype(BF16), cim_ref[d, 0, cols, :], preferred_element_type=F32))

    y = [None, None]
    project(*streams[0])
    for k, (d, n) in enumerate(streams):
        if k + 1 < len(streams):
            project(*streams[k + 1])
        recur(d, n)
        part = readout(d, n)
        y[d] = part if y[d] is None else y[d] + part
    for d in range(2):
        y_refs[d][0] = y[d].astype(BF16)


def _s5_scan(u_perm, b_re, b_im, c_re, c_im, lam_re, lam_im, lamseg_re, lamseg_im, seg_len):
    b, L, W = u_perm.shape
    chunk = SUBLANES * seg_len
    nblk, nchunk = W // LANES, L // chunk
    n_state = b_re.shape[-1]
    grid = (b, nblk, nchunk)
    fwd = pl.BlockSpec((1, chunk, LANES), lambda bi, gi, ci: (bi, ci, gi))
    bwd = pl.BlockSpec((1, chunk, LANES), lambda bi, gi, ci: (bi, nchunk - 1 - ci, gi))
    wspec = lambda a: pl.BlockSpec((2, 1) + a.shape[2:], lambda bi, gi, ci: (0, gi, 0, 0))
    out = jax.ShapeDtypeStruct((b, L, W), BF16)
    col_tile = min(MXU_DEPTH, n_state)
    n_tiles = n_state // col_tile
    state = pltpu.VMEM((2, n_tiles, chunk, col_tile), F32)
    return pl.pallas_call(
        functools.partial(_s5_scan_kernel, seg_len=seg_len, col_tile=col_tile),
        grid=grid,
        in_specs=[fwd, bwd, wspec(b_re), wspec(b_im), wspec(c_re), wspec(c_im),
                  wspec(lam_re), wspec(lam_im), wspec(lamseg_re), wspec(lamseg_im)],
        out_specs=[fwd, bwd],
        out_shape=[out, out],
        scratch_shapes=[state, state, pltpu.VMEM((2, n_tiles, SUBLANES, col_tile), F32)],
        compiler_params=_params("parallel", "parallel", "arbitrary"),
        name="s5_scan",
    )(u_perm, u_perm, b_re, b_im, c_re, c_im, lam_re, lam_im, lamseg_re, lamseg_im)


def _glu_kernel(yf_ref, yb_ref, u_ref, d_ref, w_ref, g_ref, o_ref):
    y = yf_ref[0].astype(F32) + yb_ref[0].astype(F32) + d_ref[...] * u_ref[0].astype(F32)
    z = jax.nn.gelu(y)
    gate = jnp.dot(z.astype(BF16), w_ref[...], preferred_element_type=F32)
    o_ref[0] = _rms(z * jax.nn.sigmoid(gate), g_ref[...]).astype(BF16)


def _glu(yf, yb, u, d, w_glu, gain):
    b, L, W = u.shape
    tm = min(TOKEN_TILE, L)
    tok = pl.BlockSpec((1, tm, W), lambda bi, ti: (bi, ti, 0))
    full = lambda a: pl.BlockSpec(a.shape, lambda bi, ti: (0,) * a.ndim)
    return pl.pallas_call(
        _glu_kernel,
        grid=(b, L // tm),
        in_specs=[tok, tok, tok, full(d), full(w_glu), full(gain)],
        out_specs=tok,
        out_shape=jax.ShapeDtypeStruct((b, L, W), BF16),
        compiler_params=_params("parallel", "parallel"),
        name="ssm_glu",
    )(yf, yb, u, d, w_glu, gain)


def _attn_kernel(q_ref, kt_ref, v_ref, o_ref, m_ref, acc_ref, s_ref, mnew_ref, alpha_ref, vext_ref, *,
                 group, row_block, kv_chunk):
    ki = pl.program_id(3)
    tq = q_ref.shape[2]
    tk = kt_ref.shape[2]
    per_head = tq // row_block
    nblk = group * per_head

    @pl.when(ki == 0)
    def _():
        m_ref[...] = jnp.full_like(m_ref, -jnp.inf)
        acc_ref[...] = jnp.zeros_like(acc_ref)

    vext_ref[:, :HEAD_DIM] = v_ref[0]
    vext_ref[:, HEAD_DIM:] = jnp.ones((vext_ref.shape[0], HEAD_DIM), BF16)

    def where(i):
        return i // per_head, pl.ds((i % per_head) * row_block, row_block)

    def scores(i):
        h, rows = where(i)
        s = jnp.dot(q_ref[0, h, rows, :], kt_ref[0], preferred_element_type=F32)
        s_ref[i % 2] = s
        m_prev = m_ref[h, rows, :]
        m_new = jnp.maximum(m_prev, jnp.max(s, axis=-1, keepdims=True))
        m_ref[h, rows, :] = m_new
        mnew_ref[i % 2] = m_new
        alpha_ref[i % 2] = jnp.exp2(m_prev - m_new)

    def softmax_pv(i):
        h, rows = where(i)
        slot = i % 2
        m_new = mnew_ref[slot]
        pv = None
        for c in range(tk // kv_chunk):
            cols = pl.ds(c * kv_chunk, kv_chunk)
            p = jnp.exp2(s_ref[slot, :, cols] - m_new).astype(BF16)
            part = jnp.dot(p, vext_ref[cols, :], preferred_element_type=F32)
            pv = part if pv is None else pv + part
        acc_ref[h, rows, :] = alpha_ref[slot] * acc_ref[h, rows, :] + pv

    scores(0)
    for i in range(nblk - 1):
        scores(i + 1)
        softmax_pv(i)
    softmax_pv(nblk - 1)

    @pl.when(ki == pl.num_programs(3) - 1)
    def _():
        for h in range(group):
            o_ref[0, h] = (acc_ref[h, :, :HEAD_DIM] / acc_ref[h, :, HEAD_DIM:]).astype(BF16)


def _attention(q, kt, v):
    b, heads, L, _ = q.shape
    group = heads // N_KV_HEADS
    tq, tk = min(Q_TILE, L), min(KV_TILE, L)
    row_block = min(ATTN_ROW_BLOCK, tq)
    grid = (b, N_KV_HEADS, L // tq, L // tk)
    qspec = pl.BlockSpec((1, group, tq, HEAD_DIM), lambda bi, hi, qi, ki: (bi, hi, qi, 0))
    return pl.pallas_call(
        functools.partial(_attn_kernel, group=group, row_block=row_block, kv_chunk=min(MXU_DEPTH, tk)),
        grid=grid,
        in_specs=[qspec,
                  pl.BlockSpec((1, HEAD_DIM, tk), lambda bi, hi, qi, ki: (bi, hi, ki)),
                  pl.BlockSpec((1, tk, HEAD_DIM), lambda bi, hi, qi, ki: (bi, ki, hi))],
        out_specs=qspec,
        out_shape=jax.ShapeDtypeStruct((b, heads, L, HEAD_DIM), BF16),
        scratch_shapes=[pltpu.VMEM((group, tq, 1), F32),
                        pltpu.VMEM((group, tq, 2 * HEAD_DIM), F32),
                        pltpu.VMEM((2, row_block, tk), F32),
                        pltpu.VMEM((2, row_block, 1), F32),
                        pltpu.VMEM((2, row_block, 1), F32),
                        pltpu.VMEM((tk, 2 * HEAD_DIM), BF16)],
        compiler_params=_params("parallel", "parallel", "parallel", "arbitrary"),
        name="flash_attention",
    )(q, kt, v)


def _outproj_kernel(x_ref, ys_ref, ya_ref, ga_ref, ws_ref, wa_ref, gpost_ref, gmlp_ref, x1_ref, h_ref):
    ya = jnp.concatenate([ya_ref[0, hh] for hh in range(ya_ref.shape[1])], axis=-1)
    ya = _rms(ya.astype(F32), ga_ref[...]).astype(BF16)
    mixed = (jnp.dot(ys_ref[0], ws_ref[...], preferred_element_type=F32)
             + jnp.dot(ya, wa_ref[...], preferred_element_type=F32))
    x1 = x_ref[0] + _rms(mixed, gpost_ref[...])
    x1_ref[0] = x1
    h_ref[0] = _rms(x1, gmlp_ref[...]).astype(BF16)


def _out_projection(x, y_ssm, y_att, g_att, w_ssm, w_att, g_post, g_mlp):
    b, L, d = x.shape
    tm = min(TOKEN_TILE, L)
    tok = lambda width: pl.BlockSpec((1, tm, width), lambda bi, ti: (bi, ti, 0))
    full = lambda a: pl.BlockSpec(a.shape, lambda bi, ti: (0,) * a.ndim)
    return pl.pallas_call(
        _outproj_kernel,
        grid=(b, L // tm),
        in_specs=[tok(d), tok(y_ssm.shape[-1]),
                  pl.BlockSpec((1, y_att.shape[1], tm, HEAD_DIM), lambda bi, ti: (bi, 0, ti, 0)),
                  full(g_att), full(w_ssm), full(w_att),
                  full(g_post), full(g_mlp)],
        out_specs=[tok(d), tok(d)],
        out_shape=[jax.ShapeDtypeStruct((b, L, d), F32), jax.ShapeDtypeStruct((b, L, d), BF16)],
        compiler_params=_params("parallel", "parallel"),
        name="out_projection",
    )(x, y_ssm, y_att, g_att, w_ssm, w_att, g_post, g_mlp)


def _mlp_kernel(h_ref, x1_ref, wu_ref, wd_ref, g_ref, o_ref, acc_ref):
    j = pl.program_id(2)

    @pl.when(j == 0)
    def _():
        acc_ref[...] = jnp.zeros_like(acc_ref)

    a = jnp.dot(h_ref[0], wu_ref[...], preferred_element_type=F32)
    a = jnp.square(jnp.maximum(a, 0.0)).astype(BF16)
    acc_ref[...] += jnp.dot(a, wd_ref[...], preferred_element_type=F32)

    @pl.when(j == pl.num_programs(2) - 1)
    def _():
        o_ref[0] = x1_ref[0] + _rms(acc_ref[...], g_ref[...])


def _mlp(h, x1, w_up, w_down, gain):
    b, L, d = x1.shape
    d_ff = w_up.shape[1]
    tm, tf = min(TOKEN_TILE, L), min(FF_TILE, d_ff)
    tok = pl.BlockSpec((1, tm, d), lambda bi, ti, j: (bi, ti, 0))
    return pl.pallas_call(
        _mlp_kernel,
        grid=(b, L // tm, d_ff // tf),
        in_specs=[tok, tok,
                  pl.BlockSpec((d, tf), lambda bi, ti, j: (0, j)),
                  pl.BlockSpec((tf, d), lambda bi, ti, j: (j, 0)),
                  pl.BlockSpec(gain.shape, lambda bi, ti, j: (0, 0))],
        out_specs=tok,
        out_shape=jax.ShapeDtypeStruct((b, L, d), F32),
        scratch_shapes=[pltpu.VMEM((tm, d), F32)],
        compiler_params=_params("parallel", "parallel", "arbitrary"),
        name="mlp",
    )(h, x1, w_up, w_down, gain)


def _segment_major(a, seg_len):
    b, L, W = a.shape
    n = L // (SUBLANES * seg_len)
    return a.reshape(b, n, SUBLANES, seg_len, W).transpose(0, 1, 3, 2, 4).reshape(b, L, W)


def _token_major(a, seg_len):
    b, L, W = a.shape
    n = L // (SUBLANES * seg_len)
    return a.reshape(b, n, seg_len, SUBLANES, W).transpose(0, 1, 3, 2, 4).reshape(b, L, W)


def _prepare_weights(pre_mix_norm, w_in, ssm_a_re, ssm_a_im, ssm_log_dt, ssm_b_re, ssm_b_im,
                     ssm_c_re, ssm_c_im, ssm_d, w_glu, q_norm, k_norm, ssm_out_norm, attn_out_norm,
                     w_out, post_mix_norm, pre_mlp_norm, w_up, w_down, post_mlp_norm, seg_len):
    row = lambda g: g.reshape(1, -1).astype(F32)
    two, G, P = ssm_a_re.shape
    H = ssm_b_re.shape[-1]
    ssm_w = G * H
    lam_re, lam_im, lamseg_re, lamseg_im, bb_re, bb_im = _s5_discretise(
        ssm_a_re, ssm_a_im, ssm_log_dt, ssm_b_re, ssm_b_im, seg_len)
    gpb = GROUPS_PER_BLOCK
    nblk = G // gpb
    bmat = lambda m: _block_diag(m.reshape(two * G, P, H).transpose(0, 2, 1), gpb).reshape(
        two, nblk, gpb * H, gpb * P).astype(BF16)
    cmat = lambda m: _block_diag(m.reshape(two * G, H, P).transpose(0, 2, 1), gpb).reshape(
        two, nblk, gpb * P, gpb * H).astype(BF16)
    lmat = lambda m: jnp.broadcast_to(m.reshape(two, nblk, 1, gpb * P), (two, nblk, SUBLANES, gpb * P))
    return dict(
        pre_mix_norm=row(pre_mix_norm), w_in=w_in.astype(BF16),
        b_re=bmat(bb_re), b_im=bmat(bb_im), c_re=cmat(ssm_c_re), c_im=cmat(ssm_c_im),
        lam_re=lmat(lam_re), lam_im=lmat(lam_im), lamseg_re=lmat(lamseg_re), lamseg_im=lmat(lamseg_im),
        ssm_d=row(ssm_d), w_glu=w_glu.astype(BF16), q_norm=row(q_norm), k_norm=row(k_norm),
        ssm_out_norm=row(ssm_out_norm), attn_out_norm=row(attn_out_norm),
        w_out_ssm=w_out[:ssm_w].astype(BF16), w_out_att=w_out[ssm_w:].astype(BF16),
        post_mix_norm=row(post_mix_norm), pre_mlp_norm=row(pre_mlp_norm),
        w_up=w_up.astype(BF16), w_down=w_down.astype(BF16), post_mlp_norm=row(post_mlp_norm),
        ssm_w=ssm_w, seg_len=seg_len,
    )


def _layer(x, p):
    b, L, d = x.shape
    ssm_w = p["ssm_w"]
    kv_w = N_KV_HEADS * HEAD_DIM
    attn_w = p["w_in"].shape[1] - ssm_w - 2 * kv_w
    seg_len = p["seg_len"]
    cos_tab, sin_tab = _rope_tables(L)
    u, q, k, v = _in_projection(x, p["pre_mix_norm"], p["w_in"], cos_tab, sin_tab,
                                p["q_norm"], p["k_norm"], ssm_w, attn_w, kv_w)
    u_perm = _segment_major(u, seg_len)
    yf, yb = _s5_scan(u_perm, p["b_re"], p["b_im"], p["c_re"], p["c_im"],
                      p["lam_re"], p["lam_im"], p["lamseg_re"], p["lamseg_im"], seg_len)
    y_ssm = _token_major(_glu(yf, yb, u_perm, p["ssm_d"], p["w_glu"], p["ssm_out_norm"]), seg_len)
    y_att = _attention(q, jnp.swapaxes(k, 1, 2), v)
    x1, h = _out_projection(x, y_ssm, y_att, p["attn_out_norm"], p["w_out_ssm"], p["w_out_att"],
                            p["post_mix_norm"], p["pre_mlp_norm"])
    return _mlp(h, x1, p["w_up"], p["w_down"], p["post_mlp_norm"])


def kernel(x_prompt, x_sample, pre_mix_norm, w_in, ssm_a_re, ssm_a_im, ssm_log_dt, ssm_b_re, ssm_b_im,
           ssm_c_re, ssm_c_im, ssm_d, w_glu, q_norm, k_norm, ssm_out_norm, attn_out_norm, w_out,
           post_mix_norm, pre_mlp_norm, w_up, w_down, post_mlp_norm):
    depth = w_in.shape[0]
    y_prompt, y_sample = x_prompt, x_sample
    for l in range(depth):
        p = _prepare_weights(
            pre_mix_norm[l], w_in[l], ssm_a_re[l], ssm_a_im[l], ssm_log_dt[l], ssm_b_re[l], ssm_b_im[l],
            ssm_c_re[l], ssm_c_im[l], ssm_d[l], w_glu[l], q_norm[l], k_norm[l], ssm_out_norm[l],
            attn_out_norm[l], w_out[l], post_mix_norm[l], pre_mlp_norm[l], w_up[l], w_down[l],
            post_mlp_norm[l], SEG_LEN)
        y_prompt = _layer(y_prompt, p)
        y_sample = _layer(y_sample, p)
    return (y_prompt, y_sample)
```

```python
import functools
import math

import jax
import jax.numpy as jnp
from jax import lax
from jax.experimental import pallas as pl
from jax.experimental.pallas import tpu as pltpu

F32 = jnp.float32
BF16 = jnp.bfloat16

SSM_H = 16
SSM_P = 64
HEAD_DIM = 128
N_KV_HEADS = 2
GRID_W = 64
ROPE_THETA = 10000.0
EPS = 1e-6

LANES = 128
SUBLANES = 8
MXU_DEPTH = 256
VMEM_LIMIT_BYTES = 56 * 1024 * 1024

TOKEN_TILE = 512
FF_TILE = 1024
GROUPS_PER_BLOCK = LANES // SSM_H
SEG_LEN = 128
SCAN_CHUNK = SUBLANES * SEG_LEN
SCAN_BLOCKS_PER_STEP = 2
Q_TILE = 1024
KV_TILE = 4096
ATTN_ROW_BLOCK = 256


def _params(*semantics):
    return pltpu.CompilerParams(dimension_semantics=semantics, vmem_limit_bytes=VMEM_LIMIT_BYTES)


def _rms(x, g):
    return x * lax.rsqrt(jnp.mean(x * x, axis=-1, keepdims=True) + EPS) * g


def _inproj_kernel(x_ref, g_ref, w_ref, cos_ref, sin_ref, qg_ref, kg_ref,
                   u_ref, q_ref, k_ref, v_ref, *, ssm_w, attn_w, kv_w):
    h = _rms(x_ref[0], g_ref[...]).astype(BF16)

    def proj(lo, width):
        return jnp.dot(h, w_ref[:, lo:lo + width], preferred_element_type=F32)

    cos = cos_ref[...]
    sin = sin_ref[...]
    lane = lax.broadcasted_iota(jnp.int32, cos.shape, 1)
    first_half = (lane & (HEAD_DIM // 4)) == 0

    def norm_rope(xh, gain):
        n = _rms(xh, gain)
        partner = jnp.where(first_half,
                            pltpu.roll(n, HEAD_DIM - HEAD_DIM // 4, 1),
                            pltpu.roll(n, HEAD_DIM // 4, 1))
        return n * cos + partner * sin

    scale = math.log2(math.e) / math.sqrt(HEAD_DIM)
    per_dot = MXU_DEPTH // HEAD_DIM
    for h0 in range(0, attn_w // HEAD_DIM, per_dot):
        qq = proj(ssm_w + h0 * HEAD_DIM, MXU_DEPTH)
        for j in range(per_dot):
            qh = qq[:, j * HEAD_DIM:(j + 1) * HEAD_DIM]
            q_ref[0, h0 + j] = (norm_rope(qh, qg_ref[...]) * scale).astype(BF16)
    for h0 in range(0, kv_w // HEAD_DIM, per_dot):
        kk = proj(ssm_w + attn_w + h0 * HEAD_DIM, MXU_DEPTH)
        for j in range(per_dot):
            kh = kk[:, j * HEAD_DIM:(j + 1) * HEAD_DIM]
            k_ref[0, :, (h0 + j) * HEAD_DIM:(h0 + j + 1) * HEAD_DIM] = norm_rope(kh, kg_ref[...]).astype(BF16)
    for c0 in range(0, ssm_w, MXU_DEPTH):
        u_ref[0, :, c0:c0 + MXU_DEPTH] = proj(c0, MXU_DEPTH).astype(BF16)
    v_ref[0] = proj(ssm_w + attn_w + kv_w, kv_w).astype(BF16)


def _in_projection(x, gain, w_in, cos_tab, sin_tab, q_gain, k_gain, ssm_w, attn_w, kv_w):
    b, L, d = x.shape
    tm = min(TOKEN_TILE, L)
    heads = attn_w // HEAD_DIM
    grid = (b, L // tm)
    tok = lambda width: pl.BlockSpec((1, tm, width), lambda bi, ti: (bi, ti, 0))
    full = lambda a: pl.BlockSpec(a.shape, lambda bi, ti: (0,) * a.ndim)
    tab = pl.BlockSpec((tm, HEAD_DIM), lambda bi, ti: (ti, 0))
    return pl.pallas_call(
        functools.partial(_inproj_kernel, ssm_w=ssm_w, attn_w=attn_w, kv_w=kv_w),
        grid=grid,
        in_specs=[tok(d), full(gain), full(w_in), tab, tab, full(q_gain), full(k_gain)],
        out_specs=[tok(ssm_w), pl.BlockSpec((1, heads, tm, HEAD_DIM), lambda bi, ti: (bi, 0, ti, 0)),
                   tok(kv_w), tok(kv_w)],
        out_shape=[jax.ShapeDtypeStruct((b, L, ssm_w), BF16),
                   jax.ShapeDtypeStruct((b, heads, L, HEAD_DIM), BF16),
                   jax.ShapeDtypeStruct((b, L, kv_w), BF16),
                   jax.ShapeDtypeStruct((b, L, kv_w), BF16)],
        compiler_params=_params("parallel", "parallel"),
        name="in_projection",
    )(x, gain, w_in, cos_tab, sin_tab, q_gain, k_gain)


def _rope_tables(L):
    axis_dim = HEAD_DIM // 2
    rows = L // GRID_W
    inv_freq = ROPE_THETA ** (-jnp.arange(0, axis_dim, 2, dtype=F32) / axis_dim)
    ang_r = jnp.arange(rows, dtype=F32)[:, None] * inv_freq[None, :]
    ang_c = jnp.arange(GRID_W, dtype=F32)[:, None] * inv_freq[None, :]
    rep = lambda t: jnp.repeat(t, GRID_W, axis=0)
    til = lambda t: jnp.tile(t, (rows, 1))
    cr, sr, cc, sc = rep(jnp.cos(ang_r)), rep(jnp.sin(ang_r)), til(jnp.cos(ang_c)), til(jnp.sin(ang_c))
    return (jnp.concatenate([cr, cr, cc, cc], axis=1),
            jnp.concatenate([-sr, sr, -sc, sc], axis=1))


def _zoh(lr, li, dt):
    mag = jnp.exp(lr * dt)
    lb_re, lb_im = mag * jnp.cos(li * dt), mag * jnp.sin(li * dt)
    den = lr * lr + li * li
    nr, ni = lb_re - 1.0, lb_im
    return lb_re, lb_im, (nr * lr + ni * li) / den, (ni * lr - nr * li) / den


def _s5_prep_kernel(lr_ref, li_ref, ldt_ref, lrx_ref, lix_ref, ldtx_ref, br_ref, bi_ref,
                    lam_re_ref, lam_im_ref, lamseg_re_ref, lamseg_im_ref, bb_re_ref, bb_im_ref, *, seg_len):
    lb_re, lb_im, _, _ = _zoh(lr_ref[...], li_ref[...], jnp.exp(ldt_ref[...]))
    lam_re_ref[...] = lb_re
    lam_im_ref[...] = lb_im
    pr, pi = lb_re, lb_im
    for _ in range(int(math.log2(seg_len))):
        pr, pi = pr * pr - pi * pi, 2.0 * pr * pi
    lamseg_re_ref[...] = pr
    lamseg_im_ref[...] = pi
    _, _, w_re, w_im = _zoh(lrx_ref[...], lix_ref[...], jnp.exp(ldtx_ref[...]))
    br, bi = br_ref[...], bi_ref[...]
    bb_re_ref[...] = w_re * br - w_im * bi
    bb_im_ref[...] = w_re * bi + w_im * br


def _s5_discretise(a_re, a_im, log_dt, b_re, b_im, seg_len):
    two, G, P = a_re.shape
    H = b_re.shape[-1]
    rows = two * G
    small = lambda a: a.reshape(rows, P)
    wide = lambda a: jnp.repeat(a.reshape(rows, P), H, axis=1)
    ldt = jnp.broadcast_to(log_dt.reshape(rows, 1), (rows, P))
    args = (small(a_re), small(a_im), ldt, wide(a_re), wide(a_im), jnp.repeat(ldt, H, axis=1),
            b_re.reshape(rows, P * H), b_im.reshape(rows, P * H))
    sds = lambda w: jax.ShapeDtypeStruct((rows, w), F32)
    return pl.pallas_call(
        functools.partial(_s5_prep_kernel, seg_len=seg_len),
        out_shape=[sds(P)] * 4 + [sds(P * H)] * 2,
        name="s5_discretise",
    )(*args)


def _block_diag(m, per_block):
    n, r, c = m.shape
    nb = n // per_block
    eye = jnp.eye(per_block, dtype=m.dtype)
    out = m.reshape(nb, per_block, r, 1, c) * eye[None, :, None, :, None]
    return out.reshape(nb, per_block * r, per_block * c)


def _s5_scan_kernel(uf_ref, ub_ref, bre_ref, bim_ref, cre_ref, cim_ref,
                    lre_ref, lim_ref, sre_ref, sim_ref,
                    yf_ref, yb_ref,
                    xr_ref, xi_ref, car_ref, *, seg_len, col_tile):
    @pl.when(pl.program_id(2) == 0)
    def _():
        car_ref[...] = jnp.zeros_like(car_ref)

    n_blocks = bre_ref.shape[1]
    n_tiles = xr_ref.shape[1] // n_blocks
    u_refs, y_refs = (uf_ref, ub_ref), (yf_ref, yb_ref)
    tile = lambda w, n: w * n_tiles + n
    cols_of = lambda n: pl.ds(n * col_tile, col_tile)
    lanes_of = lambda w: pl.ds(w * LANES, LANES)

    def project(d, w, n):
        u = u_refs[d][0, :, lanes_of(w)]
        xr_ref[d, tile(w, n)] = jnp.dot(u, bre_ref[d, w, :, cols_of(n)], preferred_element_type=F32)
        xi_ref[d, tile(w, n)] = jnp.dot(u, bim_ref[d, w, :, cols_of(n)], preferred_element_type=F32)

    def recur(group):
        xr = [xr_ref.at[d, tile(w, n)] for d, w, n in group]
        xi = [xi_ref.at[d, tile(w, n)] for d, w, n in group]
        order = lambda d, m: list(range(m)) if d == 0 else list(reversed(range(m)))
        lam = [(lre_ref[d, w, :, cols_of(n)], lim_ref[d, w, :, cols_of(n)]) for d, w, n in group]

        def advance(g, t, zr, zi):
            lr, li = lam[g]
            rows = pl.ds(t * SUBLANES, SUBLANES)
            return lr * zr - li * zi + xr[g][rows, :], lr * zi + li * zr + xi[g][rows, :], rows

        state = [(jnp.zeros_like(lam[g][0]), jnp.zeros_like(lam[g][0])) for g in range(len(group))]
        for k in range(seg_len):
            for g, (d, _, _) in enumerate(group):
                state[g] = advance(g, order(d, seg_len)[k], *state[g])[:2]

        for g, (d, w, n) in enumerate(group):
            er, ei = state[g]
            sr, si = sre_ref[d, w, 0:1, cols_of(n)], sim_ref[d, w, 0:1, cols_of(n)]
            cr, ci = car_ref[d, tile(w, n), 0:1], car_ref[d, tile(w, n), 1:2]
            rows_r, rows_i = [None] * SUBLANES, [None] * SUBLANES
            for j in order(d, SUBLANES):
                rows_r[j], rows_i[j] = cr, ci
                cr, ci = sr * cr - si * ci + er[j:j + 1], sr * ci + si * cr + ei[j:j + 1]
            car_ref[d, tile(w, n), 0:1] = cr
            car_ref[d, tile(w, n), 1:2] = ci
            state[g] = (jnp.concatenate(rows_r, axis=0), jnp.concatenate(rows_i, axis=0))

        for k in range(seg_len):
            for g, (d, _, _) in enumerate(group):
                zr, zi, rows = advance(g, order(d, seg_len)[k], *state[g])
                xr[g][rows, :] = zr
                xi[g][rows, :] = zi
                state[g] = (zr, zi)

    def readout(d, w, n):
        m = tile(w, n)
        return (jnp.dot(xr_ref[d, m].astype(BF16), cre_ref[d, w, cols_of(n), :], preferred_element_type=F32)
                - jnp.dot(xi_ref[d, m].astype(BF16), cim_ref[d, w, cols_of(n), :], preferred_element_type=F32))

    groups = [[(d, w, n) for d in range(2)] for w in range(n_blocks) for n in range(n_tiles)]
    y = {}
    for s in groups[0]:
        project(*s)
    for k, group in enumerate(groups):
        if k + 1 < len(groups):
            for s in groups[k + 1]:
                project(*s)
        recur(group)
        for d, w, n in group:
            part = readout(d, w, n)
            y[d, w] = part if (d, w) not in y else y[d, w] + part
            if n == n_tiles - 1:
                y_refs[d][0, :, lanes_of(w)] = y[d, w].astype(BF16)


def _s5_scan(u_perm, b_re, b_im, c_re, c_im, lam_re, lam_im, lamseg_re, lamseg_im, seg_len):
    b, L, W = u_perm.shape
    chunk = SUBLANES * seg_len
    nb = SCAN_BLOCKS_PER_STEP
    width = nb * LANES
    nstep, nchunk = W // width, L // chunk
    n_state = b_re.shape[-1]
    grid = (b, nstep, nchunk)
    fwd = pl.BlockSpec((1, chunk, width), lambda bi, gi, ci: (bi, ci, gi))
    bwd = pl.BlockSpec((1, chunk, width), lambda bi, gi, ci: (bi, nchunk - 1 - ci, gi))
    wspec = lambda a: pl.BlockSpec((2, nb) + a.shape[2:], lambda bi, gi, ci: (0, gi, 0, 0))
    out = jax.ShapeDtypeStruct((b, L, W), BF16)
    col_tile = min(MXU_DEPTH, n_state)
    n_tiles = nb * (n_state // col_tile)
    state = pltpu.VMEM((2, n_tiles, chunk, col_tile), F32)
    return pl.pallas_call(
        functools.partial(_s5_scan_kernel, seg_len=seg_len, col_tile=col_tile),
        grid=grid,
        in_specs=[fwd, bwd, wspec(b_re), wspec(b_im), wspec(c_re), wspec(c_im),
                  wspec(lam_re), wspec(lam_im), wspec(lamseg_re), wspec(lamseg_im)],
        out_specs=[fwd, bwd],
        out_shape=[out, out],
        scratch_shapes=[state, state, pltpu.VMEM((2, n_tiles, SUBLANES, col_tile), F32)],
        compiler_params=_params("parallel", "parallel", "arbitrary"),
        name="s5_scan",
    )(u_perm, u_perm, b_re, b_im, c_re, c_im, lam_re, lam_im, lamseg_re, lamseg_im)


def _glu_kernel(yf_ref, yb_ref, u_ref, d_ref, w_ref, g_ref, o_ref):
    y = yf_ref[0].astype(F32) + yb_ref[0].astype(F32) + d_ref[...] * u_ref[0].astype(F32)
    z = jax.nn.gelu(y)
    gate = jnp.dot(z.astype(BF16), w_ref[...], preferred_element_type=F32)
    o_ref[0] = _rms(z * jax.nn.sigmoid(gate), g_ref[...]).astype(BF16)


def _glu(yf, yb, u, d, w_glu, gain):
    b, L, W = u.shape
    tm = min(TOKEN_TILE, L)
    tok = pl.BlockSpec((1, tm, W), lambda bi, ti: (bi, ti, 0))
    full = lambda a: pl.BlockSpec(a.shape, lambda bi, ti: (0,) * a.ndim)
    return pl.pallas_call(
        _glu_kernel,
        grid=(b, L // tm),
        in_specs=[tok, tok, tok, full(d), full(w_glu), full(gain)],
        out_specs=tok,
        out_shape=jax.ShapeDtypeStruct((b, L, W), BF16),
        compiler_params=_params("parallel", "parallel"),
        name="ssm_glu",
    )(yf, yb, u, d, w_glu, gain)


def _attn_kernel(q_ref, kt_ref, v_ref, o_ref, m_ref, acc_ref, s_ref, mnew_ref, alpha_ref, vext_ref, *,
                 group, row_block, kv_chunk):
    ki = pl.program_id(3)
    tq = q_ref.shape[2]
    tk = kt_ref.shape[2]
    per_head = tq // row_block
    nblk = group * per_head

    @pl.when(ki == 0)
    def _():
        m_ref[...] = jnp.full_like(m_ref, -jnp.inf)
        acc_ref[...] = jnp.zeros_like(acc_ref)

    vext_ref[:, :HEAD_DIM] = v_ref[0]
    vext_ref[:, HEAD_DIM:] = jnp.ones((vext_ref.shape[0], HEAD_DIM), BF16)

    def where(i):
        return i // per_head, pl.ds((i % per_head) * row_block, row_block)

    def scores(i):
        h, rows = where(i)
        s = jnp.dot(q_ref[0, h, rows, :], kt_ref[0], preferred_element_type=F32)
        s_ref[i % 2] = s
        m_prev = m_ref[h, rows, :]
        m_new = jnp.maximum(m_prev, jnp.max(s, axis=-1, keepdims=True))
        m_ref[h, rows, :] = m_new
        mnew_ref[i % 2] = m_new
        alpha_ref[i % 2] = jnp.exp2(m_prev - m_new)

    def softmax_pv(i):
        h, rows = where(i)
        slot = i % 2
        m_new = mnew_ref[slot]
        pv = None
        for c in range(tk // kv_chunk):
            cols = pl.ds(c * kv_chunk, kv_chunk)
            p = jnp.exp2(s_ref[slot, :, cols] - m_new).astype(BF16)
            part = jnp.dot(p, vext_ref[cols, :], preferred_element_type=F32)
            pv = part if pv is None else pv + part
        acc_ref[h, rows, :] = alpha_ref[slot] * acc_ref[h, rows, :] + pv

    scores(0)
    for i in range(nblk - 1):
        scores(i + 1)
        softmax_pv(i)
    softmax_pv(nblk - 1)

    @pl.when(ki == pl.num_programs(3) - 1)
    def _():
        for h in range(group):
            o_ref[0, h] = (acc_ref[h, :, :HEAD_DIM] / acc_ref[h, :, HEAD_DIM:]).astype(BF16)


def _attention(q, kt, v):
    b, heads, L, _ = q.shape
    group = heads // N_KV_HEADS
    tq, tk = min(Q_TILE, L), min(KV_TILE, L)
    row_block = min(ATTN_ROW_BLOCK, tq)
    grid = (b, N_KV_HEADS, L // tq, L // tk)
    qspec = pl.BlockSpec((1, group, tq, HEAD_DIM), lambda bi, hi, qi, ki: (bi, hi, qi, 0))
    return pl.pallas_call(
        functools.partial(_attn_kernel, group=group, row_block=row_block, kv_chunk=min(MXU_DEPTH, tk)),
        grid=grid,
        in_specs=[qspec,
                  pl.BlockSpec((1, HEAD_DIM, tk), lambda bi, hi, qi, ki: (bi, hi, ki)),
                  pl.BlockSpec((1, tk, HEAD_DIM), lambda bi, hi, qi, ki: (bi, ki, hi))],
        out_specs=qspec,
        out_shape=jax.ShapeDtypeStruct((b, heads, L, HEAD_DIM), BF16),
        scratch_shapes=[pltpu.VMEM((group, tq, 1), F32),
                        pltpu.VMEM((group, tq, 2 * HEAD_DIM), F32),
                        pltpu.VMEM((2, row_block, tk), F32),
                        pltpu.VMEM((2, row_block, 1), F32),
                        pltpu.VMEM((2, row_block, 1), F32),
                        pltpu.VMEM((tk, 2 * HEAD_DIM), BF16)],
        compiler_params=_params("parallel", "parallel", "parallel", "arbitrary"),
        name="flash_attention",
    )(q, kt, v)


def _outproj_kernel(x_ref, ys_ref, ya_ref, ga_ref, ws_ref, wa_ref, gpost_ref, gmlp_ref, x1_ref, h_ref):
    half = x_ref.shape[1] // 2
    for r in range(2):
        rows = pl.ds(r * half, half)
        ya = jnp.concatenate([ya_ref[0, hh, rows, :] for hh in range(ya_ref.shape[1])], axis=-1)
        ya = _rms(ya.astype(F32), ga_ref[...]).astype(BF16)
        mixed = (jnp.dot(ys_ref[0, rows, :], ws_ref[...], preferred_element_type=F32)
                 + jnp.dot(ya, wa_ref[...], preferred_element_type=F32))
        x1 = x_ref[0, rows, :] + _rms(mixed, gpost_ref[...])
        x1_ref[0, rows, :] = x1
        h_ref[0, rows, :] = _rms(x1, gmlp_ref[...]).astype(BF16)


def _out_projection(x, y_ssm, y_att, g_att, w_ssm, w_att, g_post, g_mlp):
    b, L, d = x.shape
    tm = min(TOKEN_TILE, L)
    tok = lambda width: pl.BlockSpec((1, tm, width), lambda bi, ti: (bi, ti, 0))
    full = lambda a: pl.BlockSpec(a.shape, lambda bi, ti: (0,) * a.ndim)
    return pl.pallas_call(
        _outproj_kernel,
        grid=(b, L // tm),
        in_specs=[tok(d), tok(y_ssm.shape[-1]),
                  pl.BlockSpec((1, y_att.shape[1], tm, HEAD_DIM), lambda bi, ti: (bi, 0, ti, 0)),
                  full(g_att), full(w_ssm), full(w_att),
                  full(g_post), full(g_mlp)],
        out_specs=[tok(d), tok(d)],
        out_shape=[jax.ShapeDtypeStruct((b, L, d), F32), jax.ShapeDtypeStruct((b, L, d), BF16)],
        compiler_params=_params("parallel", "parallel"),
        name="out_projection",
    )(x, y_ssm, y_att, g_att, w_ssm, w_att, g_post, g_mlp)


def _mlp_kernel(h_ref, x1_ref, wu_ref, wd_ref, g_ref, o_ref, acc_ref):
    j = pl.program_id(2)

    @pl.when(j == 0)
    def _():
        acc_ref[...] = jnp.zeros_like(acc_ref)

    a = jnp.dot(h_ref[0], wu_ref[...], preferred_element_type=F32)
    a = jnp.square(jnp.maximum(a, 0.0)).astype(BF16)
    acc_ref[...] += jnp.dot(a, wd_ref[...], preferred_element_type=F32)

    @pl.when(j == pl.num_programs(2) - 1)
    def _():
        o_ref[0] = x1_ref[0] + _rms(acc_ref[...], g_ref[...])


def _mlp(h, x1, w_up, w_down, gain):
    b, L, d = x1.shape
    d_ff = w_up.shape[1]
    tm, tf = min(TOKEN_TILE, L), min(FF_TILE, d_ff)
    tok = pl.BlockSpec((1, tm, d), lambda bi, ti, j: (bi, ti, 0))
    return pl.pallas_call(
        _mlp_kernel,
        grid=(b, L // tm, d_ff // tf),
        in_specs=[tok, tok,
                  pl.BlockSpec((d, tf), lambda bi, ti, j: (0, j)),
                  pl.BlockSpec((tf, d), lambda bi, ti, j: (j, 0)),
                  pl.BlockSpec(gain.shape, lambda bi, ti, j: (0, 0))],
        out_specs=tok,
        out_shape=jax.ShapeDtypeStruct((b, L, d), F32),
        scratch_shapes=[pltpu.VMEM((tm, d), F32)],
        compiler_params=_params("parallel", "parallel", "arbitrary"),
        name="mlp",
    )(h, x1, w_up, w_down, gain)


def _segment_major(a, seg_len):
    b, L, W = a.shape
    n = L // (SUBLANES * seg_len)
    return a.reshape(b, n, SUBLANES, seg_len, W).transpose(0, 1, 3, 2, 4).reshape(b, L, W)


def _token_major(a, seg_len):
    b, L, W = a.shape
    n = L // (SUBLANES * seg_len)
    return a.reshape(b, n, seg_len, SUBLANES, W).transpose(0, 1, 3, 2, 4).reshape(b, L, W)


def _prepare_weights(pre_mix_norm, w_in, ssm_a_re, ssm_a_im, ssm_log_dt, ssm_b_re, ssm_b_im,
                     ssm_c_re, ssm_c_im, ssm_d, w_glu, q_norm, k_norm, ssm_out_norm, attn_out_norm,
                     w_out, post_mix_norm, pre_mlp_norm, w_up, w_down, post_mlp_norm, seg_len):
    row = lambda g: g.reshape(1, -1).astype(F32)
    two, G, P = ssm_a_re.shape
    H = ssm_b_re.shape[-1]
    ssm_w = G * H
    lam_re, lam_im, lamseg_re, lamseg_im, bb_re, bb_im = _s5_discretise(
        ssm_a_re, ssm_a_im, ssm_log_dt, ssm_b_re, ssm_b_im, seg_len)
    gpb = GROUPS_PER_BLOCK
    nblk = G // gpb
    bmat = lambda m: _block_diag(m.reshape(two * G, P, H).transpose(0, 2, 1), gpb).reshape(
        two, nblk, gpb * H, gpb * P).astype(BF16)
    cmat = lambda m: _block_diag(m.reshape(two * G, H, P).transpose(0, 2, 1), gpb).reshape(
        two, nblk, gpb * P, gpb * H).astype(BF16)
    lmat = lambda m: jnp.broadcast_to(m.reshape(two, nblk, 1, gpb * P), (two, nblk, SUBLANES, gpb * P))
    return dict(
        pre_mix_norm=row(pre_mix_norm), w_in=w_in.astype(BF16),
        b_re=bmat(bb_re), b_im=bmat(bb_im), c_re=cmat(ssm_c_re), c_im=cmat(ssm_c_im),
        lam_re=lmat(lam_re), lam_im=lmat(lam_im), lamseg_re=lmat(lamseg_re), lamseg_im=lmat(lamseg_im),
        ssm_d=row(ssm_d), w_glu=w_glu.astype(BF16), q_norm=row(q_norm), k_norm=row(k_norm),
        ssm_out_norm=row(ssm_out_norm), attn_out_norm=row(attn_out_norm),
        w_out_ssm=w_out[:ssm_w].astype(BF16), w_out_att=w_out[ssm_w:].astype(BF16),
        post_mix_norm=row(post_mix_norm), pre_mlp_norm=row(pre_mlp_norm),
        w_up=w_up.astype(BF16), w_down=w_down.astype(BF16), post_mlp_norm=row(post_mlp_norm),
        ssm_w=ssm_w, seg_len=seg_len,
    )


def _layer(x, p):
    b, L, d = x.shape
    ssm_w = p["ssm_w"]
    kv_w = N_KV_HEADS * HEAD_DIM
    attn_w = p["w_in"].shape[1] - ssm_w - 2 * kv_w
    seg_len = p["seg_len"]
    cos_tab, sin_tab = _rope_tables(L)
    u, q, k, v = _in_projection(x, p["pre_mix_norm"], p["w_in"], cos_tab, sin_tab,
                                p["q_norm"], p["k_norm"], ssm_w, attn_w, kv_w)
    u_perm = _segment_major(u, seg_len)
    yf, yb = _s5_scan(u_perm, p["b_re"], p["b_im"], p["c_re"], p["c_im"],
                      p["lam_re"], p["lam_im"], p["lamseg_re"], p["lamseg_im"], seg_len)
    y_ssm = _token_major(_glu(yf, yb, u_perm, p["ssm_d"], p["w_glu"], p["ssm_out_norm"]), seg_len)
    y_att = _attention(q, jnp.swapaxes(k, 1, 2), v)
    x1, h = _out_projection(x, y_ssm, y_att, p["attn_out_norm"], p["w_out_ssm"], p["w_out_att"],
                            p["post_mix_norm"], p["pre_mlp_norm"])
    return _mlp(h, x1, p["w_up"], p["w_down"], p["post_mlp_norm"])


def kernel(x_prompt, x_sample, pre_mix_norm, w_in, ssm_a_re, ssm_a_im, ssm_log_dt, ssm_b_re, ssm_b_im,
           ssm_c_re, ssm_c_im, ssm_d, w_glu, q_norm, k_norm, ssm_out_norm, attn_out_norm, w_out,
           post_mix_norm, pre_mlp_norm, w_up, w_down, post_mlp_norm):
    depth = w_in.shape[0]
    y_prompt, y_sample = x_prompt, x_sample
    for l in range(depth):
        p = _prepare_weights(
            pre_mix_norm[l], w_in[l], ssm_a_re[l], ssm_a_im[l], ssm_log_dt[l], ssm_b_re[l], ssm_b_im[l],
            ssm_c_re[l], ssm_c_im[l], ssm_d[l], w_glu[l], q_norm[l], k_norm[l], ssm_out_norm[l],
            attn_out_norm[l], w_out[l], post_mix_norm[l], pre_mlp_norm[l], w_up[l], w_down[l],
            post_mlp_norm[l], SEG_LEN)
        y_prompt = _layer(y_prompt, p)
        y_sample = _layer(y_sample, p)
    return (y_prompt, y_sample)
```

```python
import functools
import math

import jax
import jax.numpy as jnp
from jax import lax
from jax.experimental import pallas as pl
from jax.experimental.pallas import tpu as pltpu

F32 = jnp.float32
BF16 = jnp.bfloat16

SSM_H = 16
SSM_P = 64
HEAD_DIM = 128
N_KV_HEADS = 2
GRID_W = 64
ROPE_THETA = 10000.0
EPS = 1e-6

LANES = 128
SUBLANES = 8
MXU_DEPTH = 256
VMEM_LIMIT_BYTES = 56 * 1024 * 1024

TOKEN_TILE = 512
FF_TILE = 1024
GROUPS_PER_BLOCK = LANES // SSM_H
SEG_LEN = 128
SCAN_CHUNK = SUBLANES * SEG_LEN
SCAN_BLOCKS_PER_STEP = 2
Q_TILE = 1024
KV_TILE = 4096
ATTN_ROW_BLOCK = 256


def _params(*semantics):
    return pltpu.CompilerParams(dimension_semantics=semantics, vmem_limit_bytes=VMEM_LIMIT_BYTES)


def _rms(x, g):
    return x * lax.rsqrt(jnp.mean(x * x, axis=-1, keepdims=True) + EPS) * g


def _inproj_kernel(x_ref, g_ref, w_ref, cos_ref, sin_ref, qg_ref, kg_ref,
                   u_ref, q_ref, k_ref, v_ref, *, ssm_w, attn_w, kv_w):
    h = _rms(x_ref[0], g_ref[...]).astype(BF16)

    def proj(lo, width):
        return jnp.dot(h, w_ref[:, lo:lo + width], preferred_element_type=F32)

    cos = cos_ref[...]
    sin = sin_ref[...]
    lane = lax.broadcasted_iota(jnp.int32, cos.shape, 1)
    first_half = (lane & (HEAD_DIM // 4)) == 0

    def norm_rope(xh, gain):
        n = _rms(xh, gain)
        partner = jnp.where(first_half,
                            pltpu.roll(n, HEAD_DIM - HEAD_DIM // 4, 1),
                            pltpu.roll(n, HEAD_DIM // 4, 1))
        return n * cos + partner * sin

    scale = math.log2(math.e) / math.sqrt(HEAD_DIM)
    per_dot = MXU_DEPTH // HEAD_DIM
    for h0 in range(0, attn_w // HEAD_DIM, per_dot):
        qq = proj(ssm_w + h0 * HEAD_DIM, MXU_DEPTH)
        for j in range(per_dot):
            qh = qq[:, j * HEAD_DIM:(j + 1) * HEAD_DIM]
            q_ref[0, h0 + j] = (norm_rope(qh, qg_ref[...]) * scale).astype(BF16)
    for h0 in range(0, kv_w // HEAD_DIM, per_dot):
        kk = proj(ssm_w + attn_w + h0 * HEAD_DIM, MXU_DEPTH)
        for j in range(per_dot):
            kh = kk[:, j * HEAD_DIM:(j + 1) * HEAD_DIM]
            k_ref[0, :, (h0 + j) * HEAD_DIM:(h0 + j + 1) * HEAD_DIM] = norm_rope(kh, kg_ref[...]).astype(BF16)
    for c0 in range(0, ssm_w, MXU_DEPTH):
        u_ref[0, :, c0:c0 + MXU_DEPTH] = proj(c0, MXU_DEPTH).astype(BF16)
    v_ref[0] = proj(ssm_w + attn_w + kv_w, kv_w).astype(BF16)


def _in_projection(x, gain, w_in, cos_tab, sin_tab, q_gain, k_gain, ssm_w, attn_w, kv_w):
    b, L, d = x.shape
    tm = min(TOKEN_TILE, L)
    heads = attn_w // HEAD_DIM
    grid = (b, L // tm)
    tok = lambda width: pl.BlockSpec((1, tm, width), lambda bi, ti: (bi, ti, 0))
    full = lambda a: pl.BlockSpec(a.shape, lambda bi, ti: (0,) * a.ndim)
    tab = pl.BlockSpec((tm, HEAD_DIM), lambda bi, ti: (ti, 0))
    return pl.pallas_call(
        functools.partial(_inproj_kernel, ssm_w=ssm_w, attn_w=attn_w, kv_w=kv_w),
        grid=grid,
        in_specs=[tok(d), full(gain), full(w_in), tab, tab, full(q_gain), full(k_gain)],
        out_specs=[tok(ssm_w), pl.BlockSpec((1, heads, tm, HEAD_DIM), lambda bi, ti: (bi, 0, ti, 0)),
                   tok(kv_w), tok(kv_w)],
        out_shape=[jax.ShapeDtypeStruct((b, L, ssm_w), BF16),
                   jax.ShapeDtypeStruct((b, heads, L, HEAD_DIM), BF16),
                   jax.ShapeDtypeStruct((b, L, kv_w), BF16),
                   jax.ShapeDtypeStruct((b, L, kv_w), BF16)],
        compiler_params=_params("parallel", "parallel"),
        name="in_projection",
    )(x, gain, w_in, cos_tab, sin_tab, q_gain, k_gain)


def _rope_tables(L):
    axis_dim = HEAD_DIM // 2
    rows = L // GRID_W
    inv_freq = ROPE_THETA ** (-jnp.arange(0, axis_dim, 2, dtype=F32) / axis_dim)
    ang_r = jnp.arange(rows, dtype=F32)[:, None] * inv_freq[None, :]
    ang_c = jnp.arange(GRID_W, dtype=F32)[:, None] * inv_freq[None, :]
    pair = lambda t, sign: jnp.concatenate([sign * t, t], axis=1)
    rep = lambda t: jnp.repeat(t, GRID_W, axis=0)
    til = lambda t: jnp.tile(t, (rows, 1))
    cos_tab = jnp.concatenate([rep(pair(jnp.cos(ang_r), 1.0)), til(pair(jnp.cos(ang_c), 1.0))], axis=1)
    sin_tab = jnp.concatenate([rep(pair(jnp.sin(ang_r), -1.0)), til(pair(jnp.sin(ang_c), -1.0))], axis=1)
    return cos_tab, sin_tab


def _zoh(lr, li, dt):
    mag = jnp.exp(lr * dt)
    lb_re, lb_im = mag * jnp.cos(li * dt), mag * jnp.sin(li * dt)
    den = lr * lr + li * li
    nr, ni = lb_re - 1.0, lb_im
    return lb_re, lb_im, (nr * lr + ni * li) / den, (ni * lr - nr * li) / den


def _s5_prep_kernel(lr_ref, li_ref, ldt_ref, lrx_ref, lix_ref, ldtx_ref, br_ref, bi_ref,
                    lam_re_ref, lam_im_ref, lamseg_re_ref, lamseg_im_ref, bb_re_ref, bb_im_ref, *, seg_len):
    lb_re, lb_im, _, _ = _zoh(lr_ref[...], li_ref[...], jnp.exp(ldt_ref[...]))
    lam_re_ref[...] = lb_re
    lam_im_ref[...] = lb_im
    pr, pi = lb_re, lb_im
    for _ in range(int(math.log2(seg_len))):
        pr, pi = pr * pr - pi * pi, 2.0 * pr * pi
    lamseg_re_ref[...] = pr
    lamseg_im_ref[...] = pi
    _, _, w_re, w_im = _zoh(lrx_ref[...], lix_ref[...], jnp.exp(ldtx_ref[...]))
    br, bi = br_ref[...], bi_ref[...]
    bb_re_ref[...] = w_re * br - w_im * bi
    bb_im_ref[...] = w_re * bi + w_im * br


def _s5_discretise(a_re, a_im, log_dt, b_re, b_im, seg_len):
    two, G, P = a_re.shape
    H = b_re.shape[-1]
    rows = two * G
    small = lambda a: a.reshape(rows, P)
    wide = lambda a: jnp.repeat(a.reshape(rows, P), H, axis=1)
    ldt = jnp.broadcast_to(log_dt.reshape(rows, 1), (rows, P))
    args = (small(a_re), small(a_im), ldt, wide(a_re), wide(a_im), jnp.repeat(ldt, H, axis=1),
            b_re.reshape(rows, P * H), b_im.reshape(rows, P * H))
    sds = lambda w: jax.ShapeDtypeStruct((rows, w), F32)
    return pl.pallas_call(
        functools.partial(_s5_prep_kernel, seg_len=seg_len),
        out_shape=[sds(P)] * 4 + [sds(P * H)] * 2,
        name="s5_discretise",
    )(*args)


def _block_diag(m, per_block):
    n, r, c = m.shape
    nb = n // per_block
    eye = jnp.eye(per_block, dtype=m.dtype)
    out = m.reshape(nb, per_block, r, 1, c) * eye[None, :, None, :, None]
    return out.reshape(nb, per_block * r, per_block * c)


def _s5_scan_kernel(uf_ref, ub_ref, bre_ref, bim_ref, cre_ref, cim_ref,
                    lre_ref, lim_ref, sre_ref, sim_ref,
                    yf_ref, yb_ref,
                    xr_ref, xi_ref, car_ref, *, seg_len, col_tile):
    @pl.when(pl.program_id(2) == 0)
    def _():
        car_ref[...] = jnp.zeros_like(car_ref)

    n_blocks = bre_ref.shape[1]
    n_tiles = xr_ref.shape[1] // n_blocks
    u_refs, y_refs = (uf_ref, ub_ref), (yf_ref, yb_ref)
    tile = lambda w, n: w * n_tiles + n
    cols_of = lambda n: pl.ds(n * col_tile, col_tile)
    lanes_of = lambda w: pl.ds(w * LANES, LANES)

    def project(d, w, n):
        u = u_refs[d][0, :, lanes_of(w)]
        xr_ref[d, tile(w, n)] = jnp.dot(u, bre_ref[d, w, :, cols_of(n)], preferred_element_type=F32)
        xi_ref[d, tile(w, n)] = jnp.dot(u, bim_ref[d, w, :, cols_of(n)], preferred_element_type=F32)

    def recur(group):
        xr = [xr_ref.at[d, tile(w, n)] for d, w, n in group]
        xi = [xi_ref.at[d, tile(w, n)] for d, w, n in group]
        order = lambda d, m: list(range(m)) if d == 0 else list(reversed(range(m)))
        lam = [(lre_ref[d, w, :, cols_of(n)], lim_ref[d, w, :, cols_of(n)]) for d, w, n in group]

        def advance(g, t, zr, zi):
            lr, li = lam[g]
            rows = pl.ds(t * SUBLANES, SUBLANES)
            return lr * zr - li * zi + xr[g][rows, :], lr * zi + li * zr + xi[g][rows, :], rows

        state = [(jnp.zeros_like(lam[g][0]), jnp.zeros_like(lam[g][0])) for g in range(len(group))]
        for k in range(seg_len):
            for g, (d, _, _) in enumerate(group):
                state[g] = advance(g, order(d, seg_len)[k], *state[g])[:2]

        for g, (d, w, n) in enumerate(group):
            er, ei = state[g]
            sr, si = sre_ref[d, w, 0:1, cols_of(n)], sim_ref[d, w, 0:1, cols_of(n)]
            cr, ci = car_ref[d, tile(w, n), 0:1], car_ref[d, tile(w, n), 1:2]
            rows_r, rows_i = [None] * SUBLANES, [None] * SUBLANES
            for j in order(d, SUBLANES):
                rows_r[j], rows_i[j] = cr, ci
                cr, ci = sr * cr - si * ci + er[j:j + 1], sr * ci + si * cr + ei[j:j + 1]
            car_ref[d, tile(w, n), 0:1] = cr
            car_ref[d, tile(w, n), 1:2] = ci
            state[g] = (jnp.concatenate(rows_r, axis=0), jnp.concatenate(rows_i, axis=0))

        for k in range(seg_len):
            for g, (d, _, _) in enumerate(group):
                zr, zi, rows = advance(g, order(d, seg_len)[k], *state[g])
                xr[g][rows, :] = zr
                xi[g][rows, :] = zi
                state[g] = (zr, zi)

    def readout(d, w, n):
        m = tile(w, n)
        return (jnp.dot(xr_ref[d, m].astype(BF16), cre_ref[d, w, cols_of(n), :], preferred_element_type=F32)
                - jnp.dot(xi_ref[d, m].astype(BF16), cim_ref[d, w, cols_of(n), :], preferred_element_type=F32))

    groups = [[(d, w, n) for d in range(2)] for w in range(n_blocks) for n in range(n_tiles)]
    y = {}
    for s in groups[0]:
        project(*s)
    for k, group in enumerate(groups):
        if k + 1 < len(groups):
            for s in groups[k + 1]:
                project(*s)
        recur(group)
        for d, w, n in group:
            part = readout(d, w, n)
            y[d, w] = part if (d, w) not in y else y[d, w] + part
            if n == n_tiles - 1:
                y_refs[d][0, :, lanes_of(w)] = y[d, w].astype(BF16)


def _s5_scan(u_perm, b_re, b_im, c_re, c_im, lam_re, lam_im, lamseg_re, lamseg_im, seg_len):
    b, L, W = u_perm.shape
    chunk = SUBLANES * seg_len
    nb = SCAN_BLOCKS_PER_STEP
    width = nb * LANES
    nstep, nchunk = W // width, L // chunk
    n_state = b_re.shape[-1]
    grid = (b, nstep, nchunk)
    fwd = pl.BlockSpec((1, chunk, width), lambda bi, gi, ci: (bi, ci, gi))
    bwd = pl.BlockSpec((1, chunk, width), lambda bi, gi, ci: (bi, nchunk - 1 - ci, gi))
    wspec = lambda a: pl.BlockSpec((2, nb) + a.shape[2:], lambda bi, gi, ci: (0, gi, 0, 0))
    out = jax.ShapeDtypeStruct((b, L, W), BF16)
    col_tile = min(MXU_DEPTH, n_state)
    n_tiles = nb * (n_state // col_tile)
    state = pltpu.VMEM((2, n_tiles, chunk, col_tile), F32)
    return pl.pallas_call(
        functools.partial(_s5_scan_kernel, seg_len=seg_len, col_tile=col_tile),
        grid=grid,
        in_specs=[fwd, bwd, wspec(b_re), wspec(b_im), wspec(c_re), wspec(c_im),
                  wspec(lam_re), wspec(lam_im), wspec(lamseg_re), wspec(lamseg_im)],
        out_specs=[fwd, bwd],
        out_shape=[out, out],
        scratch_shapes=[state, state, pltpu.VMEM((2, n_tiles, SUBLANES, col_tile), F32)],
        compiler_params=_params("parallel", "parallel", "arbitrary"),
        name="s5_scan",
    )(u_perm, u_perm, b_re, b_im, c_re, c_im, lam_re, lam_im, lamseg_re, lamseg_im)


def _glu_kernel(yf_ref, yb_ref, u_ref, d_ref, w_ref, g_ref, o_ref):
    y = yf_ref[0].astype(F32) + yb_ref[0].astype(F32) + d_ref[...] * u_ref[0].astype(F32)
    z = jax.nn.gelu(y)
    gate = jnp.dot(z.astype(BF16), w_ref[...], preferred_element_type=F32)
    o_ref[0] = _rms(z * jax.nn.sigmoid(gate), g_ref[...]).astype(BF16)


def _glu(yf, yb, u, d, w_glu, gain):
    b, L, W = u.shape
    tm = min(TOKEN_TILE, L)
    tok = pl.BlockSpec((1, tm, W), lambda bi, ti: (bi, ti, 0))
    full = lambda a: pl.BlockSpec(a.shape, lambda bi, ti: (0,) * a.ndim)
    return pl.pallas_call(
        _glu_kernel,
        grid=(b, L // tm),
        in_specs=[tok, tok, tok, full(d), full(w_glu), full(gain)],
        out_specs=tok,
        out_shape=jax.ShapeDtypeStruct((b, L, W), BF16),
        compiler_params=_params("parallel", "parallel"),
        name="ssm_glu",
    )(yf, yb, u, d, w_glu, gain)


def _attn_kernel(q_ref, kt_ref, v_ref, o_ref, m_ref, acc_ref, s_ref, mnew_ref, alpha_ref, vext_ref, *,
                 group, row_block, kv_chunk):
    ki = pl.program_id(3)
    tq = q_ref.shape[2]
    tk = kt_ref.shape[2]
    per_head = tq // row_block
    nblk = group * per_head

    @pl.when(ki == 0)
    def _():
        m_ref[...] = jnp.full_like(m_ref, -jnp.inf)
        acc_ref[...] = jnp.zeros_like(acc_ref)

    vext_ref[:, :HEAD_DIM] = v_ref[0]
    vext_ref[:, HEAD_DIM:] = jnp.ones((vext_ref.shape[0], HEAD_DIM), BF16)

    def where(i):
        return i // per_head, pl.ds((i % per_head) * row_block, row_block)

    def scores(i):
        h, rows = where(i)
        s = jnp.dot(q_ref[0, h, rows, :], kt_ref[0], preferred_element_type=F32)
        s_ref[i % 2] = s
        m_prev = m_ref[h, rows, :]
        m_new = jnp.maximum(m_prev, jnp.max(s, axis=-1, keepdims=True))
        m_ref[h, rows, :] = m_new
        mnew_ref[i % 2] = m_new
        alpha_ref[i % 2] = jnp.exp2(m_prev - m_new)

    def softmax_pv(i):
        h, rows = where(i)
        slot = i % 2
        m_new = mnew_ref[slot]
        pv = None
        for c in range(tk // kv_chunk):
            cols = pl.ds(c * kv_chunk, kv_chunk)
            p = jnp.exp2(s_ref[slot, :, cols] - m_new).astype(BF16)
            part = jnp.dot(p, vext_ref[cols, :], preferred_element_type=F32)
            pv = part if pv is None else pv + part
        acc_ref[h, rows, :] = alpha_ref[slot] * acc_ref[h, rows, :] + pv

    scores(0)
    for i in range(nblk - 1):
        scores(i + 1)
        softmax_pv(i)
    softmax_pv(nblk - 1)

    @pl.when(ki == pl.num_programs(3) - 1)
    def _():
        for h in range(group):
            o_ref[0, h] = (acc_ref[h, :, :HEAD_DIM] / acc_ref[h, :, HEAD_DIM:]).astype(BF16)


def _attention(q, kt, v):
    b, heads, L, _ = q.shape
    group = heads // N_KV_HEADS
    tq, tk = min(Q_TILE, L), min(KV_TILE, L)
    row_block = min(ATTN_ROW_BLOCK, tq)
    grid = (b, N_KV_HEADS, L // tq, L // tk)
    qspec = pl.BlockSpec((1, group, tq, HEAD_DIM), lambda bi, hi, qi, ki: (bi, hi, qi, 0))
    return pl.pallas_call(
        functools.partial(_attn_kernel, group=group, row_block=row_block, kv_chunk=min(MXU_DEPTH, tk)),
        grid=grid,
        in_specs=[qspec,
                  pl.BlockSpec((1, HEAD_DIM, tk), lambda bi, hi, qi, ki: (bi, hi, ki)),
                  pl.BlockSpec((1, tk, HEAD_DIM), lambda bi, hi, qi, ki: (bi, ki, hi))],
        out_specs=qspec,
        out_shape=jax.ShapeDtypeStruct((b, heads, L, HEAD_DIM), BF16),
        scratch_shapes=[pltpu.VMEM((group, tq, 1), F32),
                        pltpu.VMEM((group, tq, 2 * HEAD_DIM), F32),
                        pltpu.VMEM((2, row_block, tk), F32),
                        pltpu.VMEM((2, row_block, 1), F32),
                        pltpu.VMEM((2, row_block, 1), F32),
                        pltpu.VMEM((tk, 2 * HEAD_DIM), BF16)],
        compiler_params=_params("parallel", "parallel", "parallel", "arbitrary"),
        name="flash_attention",
    )(q, kt, v)


def _outproj_kernel(x_ref, ys_ref, ya_ref, ga_ref, ws_ref, wa_ref, gpost_ref, gmlp_ref, x1_ref, h_ref):
    half = x_ref.shape[1] // 2
    for r in range(2):
        rows = pl.ds(r * half, half)
        ya = jnp.concatenate([ya_ref[0, hh, rows, :] for hh in range(ya_ref.shape[1])], axis=-1)
        ya = _rms(ya.astype(F32), ga_ref[...]).astype(BF16)
        mixed = (jnp.dot(ys_ref[0, rows, :], ws_ref[...], preferred_element_type=F32)
                 + jnp.dot(ya, wa_ref[...], preferred_element_type=F32))
        x1 = x_ref[0, rows, :] + _rms(mixed, gpost_ref[...])
        x1_ref[0, rows, :] = x1
        h_ref[0, rows, :] = _rms(x1, gmlp_ref[...]).astype(BF16)


def _out_projection(x, y_ssm, y_att, g_att, w_ssm, w_att, g_post, g_mlp):
    b, L, d = x.shape
    tm = min(TOKEN_TILE, L)
    tok = lambda width: pl.BlockSpec((1, tm, width), lambda bi, ti: (bi, ti, 0))
    full = lambda a: pl.BlockSpec(a.shape, lambda bi, ti: (0,) * a.ndim)
    return pl.pallas_call(
        _outproj_kernel,
        grid=(b, L // tm),
        in_specs=[tok(d), tok(y_ssm.shape[-1]),
                  pl.BlockSpec((1, y_att.shape[1], tm, HEAD_DIM), lambda bi, ti: (bi, 0, ti, 0)),
                  full(g_att), full(w_ssm), full(w_att),
                  full(g_post), full(g_mlp)],
        out_specs=[tok(d), tok(d)],
        out_shape=[jax.ShapeDtypeStruct((b, L, d), F32), jax.ShapeDtypeStruct((b, L, d), BF16)],
        compiler_params=_params("parallel", "parallel"),
        name="out_projection",
    )(x, y_ssm, y_att, g_att, w_ssm, w_att, g_post, g_mlp)


def _mlp_kernel(h_ref, x1_ref, wu_ref, wd_ref, g_ref, o_ref, acc_ref):
    j = pl.program_id(2)

    @pl.when(j == 0)
    def _():
        acc_ref[...] = jnp.zeros_like(acc_ref)

    a = jnp.dot(h_ref[0], wu_ref[...], preferred_element_type=F32)
    a = jnp.square(jnp.maximum(a, 0.0)).astype(BF16)
    acc_ref[...] += jnp.dot(a, wd_ref[...], preferred_element_type=F32)

    @pl.when(j == pl.num_programs(2) - 1)
    def _():
        o_ref[0] = x1_ref[0] + _rms(acc_ref[...], g_ref[...])


def _mlp(h, x1, w_up, w_down, gain):
    b, L, d = x1.shape
    d_ff = w_up.shape[1]
    tm, tf = min(TOKEN_TILE, L), min(FF_TILE, d_ff)
    tok = pl.BlockSpec((1, tm, d), lambda bi, ti, j: (bi, ti, 0))
    return pl.pallas_call(
        _mlp_kernel,
        grid=(b, L // tm, d_ff // tf),
        in_specs=[tok, tok,
                  pl.BlockSpec((d, tf), lambda bi, ti, j: (0, j)),
                  pl.BlockSpec((tf, d), lambda bi, ti, j: (j, 0)),
                  pl.BlockSpec(gain.shape, lambda bi, ti, j: (0, 0))],
        out_specs=tok,
        out_shape=jax.ShapeDtypeStruct((b, L, d), F32),
        scratch_shapes=[pltpu.VMEM((tm, d), F32)],
        compiler_params=_params("parallel", "parallel", "arbitrary"),
        name="mlp",
    )(h, x1, w_up, w_down, gain)


def _segment_major(a, seg_len):
    b, L, W = a.shape
    n = L // (SUBLANES * seg_len)
    return a.reshape(b, n, SUBLANES, seg_len, W).transpose(0, 1, 3, 2, 4).reshape(b, L, W)


def _token_major(a, seg_len):
    b, L, W = a.shape
    n = L // (SUBLANES * seg_len)
    return a.reshape(b, n, seg_len, SUBLANES, W).transpose(0, 1, 3, 2, 4).reshape(b, L, W)


def _prepare_weights(pre_mix_norm, w_in, ssm_a_re, ssm_a_im, ssm_log_dt, ssm_b_re, ssm_b_im,
                     ssm_c_re, ssm_c_im, ssm_d, w_glu, q_norm, k_norm, ssm_out_norm, attn_out_norm,
                     w_out, post_mix_norm, pre_mlp_norm, w_up, w_down, post_mlp_norm, seg_len):
    row = lambda g: g.reshape(1, -1).astype(F32)
    two, G, P = ssm_a_re.shape
    H = ssm_b_re.shape[-1]
    ssm_w = G * H
    lam_re, lam_im, lamseg_re, lamseg_im, bb_re, bb_im = _s5_discretise(
        ssm_a_re, ssm_a_im, ssm_log_dt, ssm_b_re, ssm_b_im, seg_len)
    gpb = GROUPS_PER_BLOCK
    nblk = G // gpb
    bmat = lambda m: _block_diag(m.reshape(two * G, P, H).transpose(0, 2, 1), gpb).reshape(
        two, nblk, gpb * H, gpb * P).astype(BF16)
    cmat = lambda m: _block_diag(m.reshape(two * G, H, P).transpose(0, 2, 1), gpb).reshape(
        two, nblk, gpb * P, gpb * H).astype(BF16)
    lmat = lambda m: jnp.broadcast_to(m.reshape(two, nblk, 1, gpb * P), (two, nblk, SUBLANES, gpb * P))
    return dict(
        pre_mix_norm=row(pre_mix_norm), w_in=w_in.astype(BF16),
        b_re=bmat(bb_re), b_im=bmat(bb_im), c_re=cmat(ssm_c_re), c_im=cmat(ssm_c_im),
        lam_re=lmat(lam_re), lam_im=lmat(lam_im), lamseg_re=lmat(lamseg_re), lamseg_im=lmat(lamseg_im),
        ssm_d=row(ssm_d), w_glu=w_glu.astype(BF16), q_norm=row(q_norm), k_norm=row(k_norm),
        ssm_out_norm=row(ssm_out_norm), attn_out_norm=row(attn_out_norm),
        w_out_ssm=w_out[:ssm_w].astype(BF16), w_out_att=w_out[ssm_w:].astype(BF16),
        post_mix_norm=row(post_mix_norm), pre_mlp_norm=row(pre_mlp_norm),
        w_up=w_up.astype(BF16), w_down=w_down.astype(BF16), post_mlp_norm=row(post_mlp_norm),
        ssm_w=ssm_w, seg_len=seg_len,
    )


def _layer(x, p, rope_tables):
    ssm_w = p["ssm_w"]
    kv_w = N_KV_HEADS * HEAD_DIM
    attn_w = p["w_in"].shape[1] - ssm_w - 2 * kv_w
    seg_len = p["seg_len"]
    cos_tab, sin_tab = rope_tables
    u, q, k, v = _in_projection(x, p["pre_mix_norm"], p["w_in"], cos_tab, sin_tab,
                                p["q_norm"], p["k_norm"], ssm_w, attn_w, kv_w)
    u_perm = _segment_major(u, seg_len)
    yf, yb = _s5_scan(u_perm, p["b_re"], p["b_im"], p["c_re"], p["c_im"],
                      p["lam_re"], p["lam_im"], p["lamseg_re"], p["lamseg_im"], seg_len)
    y_ssm = _token_major(_glu(yf, yb, u_perm, p["ssm_d"], p["w_glu"], p["ssm_out_norm"]), seg_len)
    y_att = _attention(q, jnp.swapaxes(k, 1, 2), v)
    x1, h = _out_projection(x, y_ssm, y_att, p["attn_out_norm"], p["w_out_ssm"], p["w_out_att"],
                            p["post_mix_norm"], p["pre_mlp_norm"])
    return _mlp(h, x1, p["w_up"], p["w_down"], p["post_mlp_norm"])


def kernel(x_prompt, x_sample, pre_mix_norm, w_in, ssm_a_re, ssm_a_im, ssm_log_dt, ssm_b_re, ssm_b_im,
           ssm_c_re, ssm_c_im, ssm_d, w_glu, q_norm, k_norm, ssm_out_norm, attn_out_norm, w_out,
           post_mix_norm, pre_mlp_norm, w_up, w_down, post_mlp_norm):
    depth = w_in.shape[0]
    y_prompt, y_sample = x_prompt, x_sample
    rope_tables = _rope_tables(max(x_prompt.shape[1], x_sample.shape[1]))
    for l in range(depth):
        p = _prepare_weights(
            pre_mix_norm[l], w_in[l], ssm_a_re[l], ssm_a_im[l], ssm_log_dt[l], ssm_b_re[l], ssm_b_im[l],
            ssm_c_re[l], ssm_c_im[l], ssm_d[l], w_glu[l], q_norm[l], k_norm[l], ssm_out_norm[l],
            attn_out_norm[l], w_out[l], post_mix_norm[l], pre_mlp_norm[l], w_up[l], w_down[l],
            post_mlp_norm[l], SEG_LEN)
        y_prompt = _layer(y_prompt, p, rope_tables)
        y_sample = _layer(y_sample, p, rope_tables)
    return (y_prompt, y_sample)
```

```python
import functools
import math

import jax
import jax.numpy as jnp
from jax import lax
from jax.experimental import pallas as pl
from jax.experimental.pallas import tpu as pltpu

F32 = jnp.float32
BF16 = jnp.bfloat16

SSM_H = 16
SSM_P = 64
HEAD_DIM = 128
N_KV_HEADS = 2
GRID_W = 64
ROPE_THETA = 10000.0
EPS = 1e-6

LANES = 128
SUBLANES = 8
MXU_DEPTH = 256
VMEM_LIMIT_BYTES = 56 * 1024 * 1024

TOKEN_TILE = 512
FF_TILE = 1024
GROUPS_PER_BLOCK = LANES // SSM_H
SEG_LEN = 128
SCAN_CHUNK = SUBLANES * SEG_LEN
SCAN_BLOCKS_PER_STEP = 4
Q_TILE = 1024
KV_TILE = 4096
ATTN_ROW_BLOCK = 256


def _params(*semantics):
    return pltpu.CompilerParams(dimension_semantics=semantics, vmem_limit_bytes=VMEM_LIMIT_BYTES)


def _rms(x, g):
    return x * lax.rsqrt(jnp.mean(x * x, axis=-1, keepdims=True) + EPS) * g


def _inproj_kernel(x_ref, g_ref, w_ref, cos_ref, sin_ref, qg_ref, kg_ref,
                   u_ref, q_ref, k_ref, v_ref, *, ssm_w, attn_w, kv_w):
    h = _rms(x_ref[0], g_ref[...]).astype(BF16)

    def proj(lo, width):
        return jnp.dot(h, w_ref[:, lo:lo + width], preferred_element_type=F32)

    cos = cos_ref[...]
    sin = sin_ref[...]
    lane = lax.broadcasted_iota(jnp.int32, cos.shape, 1)
    first_half = (lane & (HEAD_DIM // 4)) == 0

    def norm_rope(xh, gain):
        n = _rms(xh, gain)
        partner = jnp.where(first_half,
                            pltpu.roll(n, HEAD_DIM - HEAD_DIM // 4, 1),
                            pltpu.roll(n, HEAD_DIM // 4, 1))
        return n * cos + partner * sin

    scale = math.log2(math.e) / math.sqrt(HEAD_DIM)
    per_dot = MXU_DEPTH // HEAD_DIM
    for h0 in range(0, attn_w // HEAD_DIM, per_dot):
        qq = proj(ssm_w + h0 * HEAD_DIM, MXU_DEPTH)
        for j in range(per_dot):
            qh = qq[:, j * HEAD_DIM:(j + 1) * HEAD_DIM]
            q_ref[0, h0 + j] = (norm_rope(qh, qg_ref[...]) * scale).astype(BF16)
    for h0 in range(0, kv_w // HEAD_DIM, per_dot):
        kk = proj(ssm_w + attn_w + h0 * HEAD_DIM, MXU_DEPTH)
        for j in range(per_dot):
            kh = kk[:, j * HEAD_DIM:(j + 1) * HEAD_DIM]
            k_ref[0, :, (h0 + j) * HEAD_DIM:(h0 + j + 1) * HEAD_DIM] = norm_rope(kh, kg_ref[...]).astype(BF16)
    for c0 in range(0, ssm_w, MXU_DEPTH):
        u_ref[0, :, c0:c0 + MXU_DEPTH] = proj(c0, MXU_DEPTH).astype(BF16)
    v_ref[0] = proj(ssm_w + attn_w + kv_w, kv_w).astype(BF16)


def _in_projection(x, gain, w_in, cos_tab, sin_tab, q_gain, k_gain, ssm_w, attn_w, kv_w):
    b, L, d = x.shape
    tm = min(TOKEN_TILE, L)
    heads = attn_w // HEAD_DIM
    grid = (b, L // tm)
    tok = lambda width: pl.BlockSpec((1, tm, width), lambda bi, ti: (bi, ti, 0))
    full = lambda a: pl.BlockSpec(a.shape, lambda bi, ti: (0,) * a.ndim)
    tab = pl.BlockSpec((tm, HEAD_DIM), lambda bi, ti: (ti, 0))
    return pl.pallas_call(
        functools.partial(_inproj_kernel, ssm_w=ssm_w, attn_w=attn_w, kv_w=kv_w),
        grid=grid,
        in_specs=[tok(d), full(gain), full(w_in), tab, tab, full(q_gain), full(k_gain)],
        out_specs=[tok(ssm_w), pl.BlockSpec((1, heads, tm, HEAD_DIM), lambda bi, ti: (bi, 0, ti, 0)),
                   tok(kv_w), tok(kv_w)],
        out_shape=[jax.ShapeDtypeStruct((b, L, ssm_w), BF16),
                   jax.ShapeDtypeStruct((b, heads, L, HEAD_DIM), BF16),
                   jax.ShapeDtypeStruct((b, L, kv_w), BF16),
                   jax.ShapeDtypeStruct((b, L, kv_w), BF16)],
        compiler_params=_params("parallel", "parallel"),
        name="in_projection",
    )(x, gain, w_in, cos_tab, sin_tab, q_gain, k_gain)


def _rope_tables(L):
    axis_dim = HEAD_DIM // 2
    rows = L // GRID_W
    inv_freq = ROPE_THETA ** (-jnp.arange(0, axis_dim, 2, dtype=F32) / axis_dim)
    ang_r = jnp.arange(rows, dtype=F32)[:, None] * inv_freq[None, :]
    ang_c = jnp.arange(GRID_W, dtype=F32)[:, None] * inv_freq[None, :]
    pair = lambda t, sign: jnp.concatenate([sign * t, t], axis=1)
    rep = lambda t: jnp.repeat(t, GRID_W, axis=0)
    til = lambda t: jnp.tile(t, (rows, 1))
    cos_tab = jnp.concatenate([rep(pair(jnp.cos(ang_r), 1.0)), til(pair(jnp.cos(ang_c), 1.0))], axis=1)
    sin_tab = jnp.concatenate([rep(pair(jnp.sin(ang_r), -1.0)), til(pair(jnp.sin(ang_c), -1.0))], axis=1)
    return cos_tab, sin_tab


def _zoh(lr, li, dt):
    mag = jnp.exp(lr * dt)
    lb_re, lb_im = mag * jnp.cos(li * dt), mag * jnp.sin(li * dt)
    den = lr * lr + li * li
    nr, ni = lb_re - 1.0, lb_im
    return lb_re, lb_im, (nr * lr + ni * li) / den, (ni * lr - nr * li) / den


def _s5_prep_kernel(lr_ref, li_ref, ldt_ref, lrx_ref, lix_ref, ldtx_ref, br_ref, bi_ref,
                    lam_re_ref, lam_im_ref, lamseg_re_ref, lamseg_im_ref, bb_re_ref, bb_im_ref, *, seg_len):
    lb_re, lb_im, _, _ = _zoh(lr_ref[...], li_ref[...], jnp.exp(ldt_ref[...]))
    lam_re_ref[...] = lb_re
    lam_im_ref[...] = lb_im
    pr, pi = lb_re, lb_im
    for _ in range(int(math.log2(seg_len))):
        pr, pi = pr * pr - pi * pi, 2.0 * pr * pi
    lamseg_re_ref[...] = pr
    lamseg_im_ref[...] = pi
    _, _, w_re, w_im = _zoh(lrx_ref[...], lix_ref[...], jnp.exp(ldtx_ref[...]))
    br, bi = br_ref[...], bi_ref[...]
    bb_re_ref[...] = w_re * br - w_im * bi
    bb_im_ref[...] = w_re * bi + w_im * br


def _s5_discretise(a_re, a_im, log_dt, b_re, b_im, seg_len):
    two, G, P = a_re.shape
    H = b_re.shape[-1]
    rows = two * G
    small = lambda a: a.reshape(rows, P)
    wide = lambda a: jnp.repeat(a.reshape(rows, P), H, axis=1)
    ldt = jnp.broadcast_to(log_dt.reshape(rows, 1), (rows, P))
    args = (small(a_re), small(a_im), ldt, wide(a_re), wide(a_im), jnp.repeat(ldt, H, axis=1),
            b_re.reshape(rows, P * H), b_im.reshape(rows, P * H))
    sds = lambda w: jax.ShapeDtypeStruct((rows, w), F32)
    return pl.pallas_call(
        functools.partial(_s5_prep_kernel, seg_len=seg_len),
        out_shape=[sds(P)] * 4 + [sds(P * H)] * 2,
        name="s5_discretise",
    )(*args)


def _block_diag(m, per_block):
    n, r, c = m.shape
    nb = n // per_block
    eye = jnp.eye(per_block, dtype=m.dtype)
    out = m.reshape(nb, per_block, r, 1, c) * eye[None, :, None, :, None]
    return out.reshape(nb, per_block * r, per_block * c)


def _s5_scan_kernel(uf_ref, ub_ref, bre_ref, bim_ref, cre_ref, cim_ref,
                    lre_ref, lim_ref, sre_ref, sim_ref,
                    yf_ref, yb_ref,
                    xr_ref, xi_ref, car_ref, *, seg_len, col_tile):
    @pl.when(pl.program_id(2) == 0)
    def _():
        car_ref[...] = jnp.zeros_like(car_ref)

    n_blocks = bre_ref.shape[1]
    n_tiles = xr_ref.shape[1] // n_blocks
    u_refs, y_refs = (uf_ref, ub_ref), (yf_ref, yb_ref)
    tile = lambda w, n: w * n_tiles + n
    cols_of = lambda n: pl.ds(n * col_tile, col_tile)
    lanes_of = lambda w: pl.ds(w * LANES, LANES)

    def project(d, w, n):
        u = u_refs[d][0, :, lanes_of(w)]
        xr_ref[d, tile(w, n)] = jnp.dot(u, bre_ref[d, w, :, cols_of(n)], preferred_element_type=F32)
        xi_ref[d, tile(w, n)] = jnp.dot(u, bim_ref[d, w, :, cols_of(n)], preferred_element_type=F32)

    def recur(group):
        xr = [xr_ref.at[d, tile(w, n)] for d, w, n in group]
        xi = [xi_ref.at[d, tile(w, n)] for d, w, n in group]
        order = lambda d, m: list(range(m)) if d == 0 else list(reversed(range(m)))
        lam = [(lre_ref[d, w, :, cols_of(n)], lim_ref[d, w, :, cols_of(n)]) for d, w, n in group]

        def advance(g, t, zr, zi):
            lr, li = lam[g]
            rows = pl.ds(t * SUBLANES, SUBLANES)
            return lr * zr - li * zi + xr[g][rows, :], lr * zi + li * zr + xi[g][rows, :], rows

        state = [(jnp.zeros_like(lam[g][0]), jnp.zeros_like(lam[g][0])) for g in range(len(group))]
        for k in range(seg_len):
            for g, (d, _, _) in enumerate(group):
                state[g] = advance(g, order(d, seg_len)[k], *state[g])[:2]

        for g, (d, w, n) in enumerate(group):
            er, ei = state[g]
            sr, si = sre_ref[d, w, 0:1, cols_of(n)], sim_ref[d, w, 0:1, cols_of(n)]
            cr, ci = car_ref[d, tile(w, n), 0:1], car_ref[d, tile(w, n), 1:2]
            rows_r, rows_i = [None] * SUBLANES, [None] * SUBLANES
            for j in order(d, SUBLANES):
                rows_r[j], rows_i[j] = cr, ci
                cr, ci = sr * cr - si * ci + er[j:j + 1], sr * ci + si * cr + ei[j:j + 1]
            car_ref[d, tile(w, n), 0:1] = cr
            car_ref[d, tile(w, n), 1:2] = ci
            state[g] = (jnp.concatenate(rows_r, axis=0), jnp.concatenate(rows_i, axis=0))

        for k in range(seg_len):
            for g, (d, _, _) in enumerate(group):
                zr, zi, rows = advance(g, order(d, seg_len)[k], *state[g])
                xr[g][rows, :] = zr
                xi[g][rows, :] = zi
                state[g] = (zr, zi)

    def readout(d, w, n):
        m = tile(w, n)
        return (jnp.dot(xr_ref[d, m].astype(BF16), cre_ref[d, w, cols_of(n), :], preferred_element_type=F32)
                - jnp.dot(xi_ref[d, m].astype(BF16), cim_ref[d, w, cols_of(n), :], preferred_element_type=F32))

    groups = [[(d, w, n) for d in range(2)] for w in range(n_blocks) for n in range(n_tiles)]
    y = {}
    for s in groups[0]:
        project(*s)
    for k, group in enumerate(groups):
        if k + 1 < len(groups):
            for s in groups[k + 1]:
                project(*s)
        recur(group)
        for d, w, n in group:
            part = readout(d, w, n)
            y[d, w] = part if (d, w) not in y else y[d, w] + part
            if n == n_tiles - 1:
                y_refs[d][0, :, lanes_of(w)] = y[d, w].astype(BF16)


def _s5_scan(u_perm, b_re, b_im, c_re, c_im, lam_re, lam_im, lamseg_re, lamseg_im, seg_len):
    b, L, W = u_perm.shape
    chunk = SUBLANES * seg_len
    nb = SCAN_BLOCKS_PER_STEP
    width = nb * LANES
    nstep, nchunk = W // width, L // chunk
    n_state = b_re.shape[-1]
    grid = (b, nstep, nchunk)
    fwd = pl.BlockSpec((1, chunk, width), lambda bi, gi, ci: (bi, ci, gi))
    bwd = pl.BlockSpec((1, chunk, width), lambda bi, gi, ci: (bi, nchunk - 1 - ci, gi))
    wspec = lambda a: pl.BlockSpec((2, nb) + a.shape[2:], lambda bi, gi, ci: (0, gi, 0, 0))
    out = jax.ShapeDtypeStruct((b, L, W), BF16)
    col_tile = min(MXU_DEPTH, n_state)
    n_tiles = nb * (n_state // col_tile)
    state = pltpu.VMEM((2, n_tiles, chunk, col_tile), F32)
    return pl.pallas_call(
        functools.partial(_s5_scan_kernel, seg_len=seg_len, col_tile=col_tile),
        grid=grid,
        in_specs=[fwd, bwd, wspec(b_re), wspec(b_im), wspec(c_re), wspec(c_im),
                  wspec(lam_re), wspec(lam_im), wspec(lamseg_re), wspec(lamseg_im)],
        out_specs=[fwd, bwd],
        out_shape=[out, out],
        scratch_shapes=[state, state, pltpu.VMEM((2, n_tiles, SUBLANES, col_tile), F32)],
        compiler_params=_params("parallel", "parallel", "arbitrary"),
        name="s5_scan",
    )(u_perm, u_perm, b_re, b_im, c_re, c_im, lam_re, lam_im, lamseg_re, lamseg_im)


def _glu_kernel(yf_ref, yb_ref, u_ref, d_ref, w_ref, g_ref, o_ref):
    y = yf_ref[0].astype(F32) + yb_ref[0].astype(F32) + d_ref[...] * u_ref[0].astype(F32)
    z = jax.nn.gelu(y)
    gate = jnp.dot(z.astype(BF16), w_ref[...], preferred_element_type=F32)
    o_ref[0] = _rms(z * jax.nn.sigmoid(gate), g_ref[...]).astype(BF16)


def _glu(yf, yb, u, d, w_glu, gain):
    b, L, W = u.shape
    tm = min(TOKEN_TILE, L)
    tok = pl.BlockSpec((1, tm, W), lambda bi, ti: (bi, ti, 0))
    full = lambda a: pl.BlockSpec(a.shape, lambda bi, ti: (0,) * a.ndim)
    return pl.pallas_call(
        _glu_kernel,
        grid=(b, L // tm),
        in_specs=[tok, tok, tok, full(d), full(w_glu), full(gain)],
        out_specs=tok,
        out_shape=jax.ShapeDtypeStruct((b, L, W), BF16),
        compiler_params=_params("parallel", "parallel"),
        name="ssm_glu",
    )(yf, yb, u, d, w_glu, gain)


def _attn_kernel(q_ref, kt_ref, v_ref, o_ref, m_ref, acc_ref, s_ref, mnew_ref, alpha_ref, vext_ref, *,
                 group, row_block, kv_chunk):
    ki = pl.program_id(3)
    tq = q_ref.shape[2]
    tk = kt_ref.shape[2]
    per_head = tq // row_block
    nblk = group * per_head

    @pl.when(ki == 0)
    def _():
        m_ref[...] = jnp.full_like(m_ref, -jnp.inf)
        acc_ref[...] = jnp.zeros_like(acc_ref)

    vext_ref[:, :HEAD_DIM] = v_ref[0]
    vext_ref[:, HEAD_DIM:] = jnp.ones((vext_ref.shape[0], HEAD_DIM), BF16)

    def where(i):
        return i // per_head, pl.ds((i % per_head) * row_block, row_block)

    def scores(i):
        h, rows = where(i)
        s = jnp.dot(q_ref[0, h, rows, :], kt_ref[0], preferred_element_type=F32)
        s_ref[i % 2] = s
        m_prev = m_ref[h, rows, :]
        m_new = jnp.maximum(m_prev, jnp.max(s, axis=-1, keepdims=True))
        m_ref[h, rows, :] = m_new
        mnew_ref[i % 2] = m_new
        alpha_ref[i % 2] = jnp.exp2(m_prev - m_new)

    def softmax_pv(i):
        h, rows = where(i)
        slot = i % 2
        m_new = mnew_ref[slot]
        pv = None
        for c in range(tk // kv_chunk):
            cols = pl.ds(c * kv_chunk, kv_chunk)
            p = jnp.exp2(s_ref[slot, :, cols] - m_new).astype(BF16)
            part = jnp.dot(p, vext_ref[cols, :], preferred_element_type=F32)
            pv = part if pv is None else pv + part
        acc_ref[h, rows, :] = alpha_ref[slot] * acc_ref[h, rows, :] + pv

    scores(0)
    for i in range(nblk - 1):
        scores(i + 1)
        softmax_pv(i)
    softmax_pv(nblk - 1)

    @pl.when(ki == pl.num_programs(3) - 1)
    def _():
        for h in range(group):
            o_ref[0, h] = (acc_ref[h, :, :HEAD_DIM] / acc_ref[h, :, HEAD_DIM:]).astype(BF16)


def _attention(q, kt, v):
    b, heads, L, _ = q.shape
    group = heads // N_KV_HEADS
    tq, tk = min(Q_TILE, L), min(KV_TILE, L)
    row_block = min(ATTN_ROW_BLOCK, tq)
    grid = (b, N_KV_HEADS, L // tq, L // tk)
    qspec = pl.BlockSpec((1, group, tq, HEAD_DIM), lambda bi, hi, qi, ki: (bi, hi, qi, 0))
    return pl.pallas_call(
        functools.partial(_attn_kernel, group=group, row_block=row_block, kv_chunk=min(MXU_DEPTH, tk)),
        grid=grid,
        in_specs=[qspec,
                  pl.BlockSpec((1, HEAD_DIM, tk), lambda bi, hi, qi, ki: (bi, hi, ki)),
                  pl.BlockSpec((1, tk, HEAD_DIM), lambda bi, hi, qi, ki: (bi, ki, hi))],
        out_specs=qspec,
        out_shape=jax.ShapeDtypeStruct((b, heads, L, HEAD_DIM), BF16),
        scratch_shapes=[pltpu.VMEM((group, tq, 1), F32),
                        pltpu.VMEM((group, tq, 2 * HEAD_DIM), F32),
                        pltpu.VMEM((2, row_block, tk), F32),
                        pltpu.VMEM((2, row_block, 1), F32),
                        pltpu.VMEM((2, row_block, 1), F32),
                        pltpu.VMEM((tk, 2 * HEAD_DIM), BF16)],
        compiler_params=_params("parallel", "parallel", "parallel", "arbitrary"),
        name="flash_attention",
    )(q, kt, v)


def _outproj_kernel(x_ref, ys_ref, ya_ref, ga_ref, ws_ref, wa_ref, gpost_ref, gmlp_ref, x1_ref, h_ref):
    half = x_ref.shape[1] // 2
    for r in range(2):
        rows = pl.ds(r * half, half)
        ya = jnp.concatenate([ya_ref[0, hh, rows, :] for hh in range(ya_ref.shape[1])], axis=-1)
        ya = _rms(ya.astype(F32), ga_ref[...]).astype(BF16)
        mixed = (jnp.dot(ys_ref[0, rows, :], ws_ref[...], preferred_element_type=F32)
                 + jnp.dot(ya, wa_ref[...], preferred_element_type=F32))
        x1 = x_ref[0, rows, :] + _rms(mixed, gpost_ref[...])
        x1_ref[0, rows, :] = x1
        h_ref[0, rows, :] = _rms(x1, gmlp_ref[...]).astype(BF16)


def _out_projection(x, y_ssm, y_att, g_att, w_ssm, w_att, g_post, g_mlp):
    b, L, d = x.shape
    tm = min(TOKEN_TILE, L)
    tok = lambda width: pl.BlockSpec((1, tm, width), lambda bi, ti: (bi, ti, 0))
    full = lambda a: pl.BlockSpec(a.shape, lambda bi, ti: (0,) * a.ndim)
    return pl.pallas_call(
        _outproj_kernel,
        grid=(b, L // tm),
        in_specs=[tok(d), tok(y_ssm.shape[-1]),
                  pl.BlockSpec((1, y_att.shape[1], tm, HEAD_DIM), lambda bi, ti: (bi, 0, ti, 0)),
                  full(g_att), full(w_ssm), full(w_att),
                  full(g_post), full(g_mlp)],
        out_specs=[tok(d), tok(d)],
        out_shape=[jax.ShapeDtypeStruct((b, L, d), F32), jax.ShapeDtypeStruct((b, L, d), BF16)],
        compiler_params=_params("parallel", "parallel"),
        name="out_projection",
    )(x, y_ssm, y_att, g_att, w_ssm, w_att, g_post, g_mlp)


def _mlp_kernel(h_ref, x1_ref, wu_ref, wd_ref, g_ref, o_ref, acc_ref):
    j = pl.program_id(2)

    @pl.when(j == 0)
    def _():
        acc_ref[...] = jnp.zeros_like(acc_ref)

    a = jnp.dot(h_ref[0], wu_ref[...], preferred_element_type=F32)
    a = jnp.square(jnp.maximum(a, 0.0)).astype(BF16)
    acc_ref[...] += jnp.dot(a, wd_ref[...], preferred_element_type=F32)

    @pl.when(j == pl.num_programs(2) - 1)
    def _():
        o_ref[0] = x1_ref[0] + _rms(acc_ref[...], g_ref[...])


def _mlp(h, x1, w_up, w_down, gain):
    b, L, d = x1.shape
    d_ff = w_up.shape[1]
    tm, tf = min(TOKEN_TILE, L), min(FF_TILE, d_ff)
    tok = pl.BlockSpec((1, tm, d), lambda bi, ti, j: (bi, ti, 0))
    return pl.pallas_call(
        _mlp_kernel,
        grid=(b, L // tm, d_ff // tf),
        in_specs=[tok, tok,
                  pl.BlockSpec((d, tf), lambda bi, ti, j: (0, j)),
                  pl.BlockSpec((tf, d), lambda bi, ti, j: (j, 0)),
                  pl.BlockSpec(gain.shape, lambda bi, ti, j: (0, 0))],
        out_specs=tok,
        out_shape=jax.ShapeDtypeStruct((b, L, d), F32),
        scratch_shapes=[pltpu.VMEM((tm, d), F32)],
        compiler_params=_params("parallel", "parallel", "arbitrary"),
        name="mlp",
    )(h, x1, w_up, w_down, gain)


def _segment_major(a, seg_len):
    b, L, W = a.shape
    n = L // (SUBLANES * seg_len)
    return a.reshape(b, n, SUBLANES, seg_len, W).transpose(0, 1, 3, 2, 4).reshape(b, L, W)


def _token_major(a, seg_len):
    b, L, W = a.shape
    n = L // (SUBLANES * seg_len)
    return a.reshape(b, n, seg_len, SUBLANES, W).transpose(0, 1, 3, 2, 4).reshape(b, L, W)


def _prepare_weights(pre_mix_norm, w_in, ssm_a_re, ssm_a_im, ssm_log_dt, ssm_b_re, ssm_b_im,
                     ssm_c_re, ssm_c_im, ssm_d, w_glu, q_norm, k_norm, ssm_out_norm, attn_out_norm,
                     w_out, post_mix_norm, pre_mlp_norm, w_up, w_down, post_mlp_norm, seg_len):
    row = lambda g: g.reshape(1, -1).astype(F32)
    two, G, P = ssm_a_re.shape
    H = ssm_b_re.shape[-1]
    ssm_w = G * H
    lam_re, lam_im, lamseg_re, lamseg_im, bb_re, bb_im = _s5_discretise(
        ssm_a_re, ssm_a_im, ssm_log_dt, ssm_b_re, ssm_b_im, seg_len)
    gpb = GROUPS_PER_BLOCK
    nblk = G // gpb
    bmat = lambda m: _block_diag(m.reshape(two * G, P, H).transpose(0, 2, 1), gpb).reshape(
        two, nblk, gpb * H, gpb * P).astype(BF16)
    cmat = lambda m: _block_diag(m.reshape(two * G, H, P).transpose(0, 2, 1), gpb).reshape(
        two, nblk, gpb * P, gpb * H).astype(BF16)
    lmat = lambda m: jnp.broadcast_to(m.reshape(two, nblk, 1, gpb * P), (two, nblk, SUBLANES, gpb * P))
    return dict(
        pre_mix_norm=row(pre_mix_norm), w_in=w_in.astype(BF16),
        b_re=bmat(bb_re), b_im=bmat(bb_im), c_re=cmat(ssm_c_re), c_im=cmat(ssm_c_im),
        lam_re=lmat(lam_re), lam_im=lmat(lam_im), lamseg_re=lmat(lamseg_re), lamseg_im=lmat(lamseg_im),
        ssm_d=row(ssm_d), w_glu=w_glu.astype(BF16), q_norm=row(q_norm), k_norm=row(k_norm),
        ssm_out_norm=row(ssm_out_norm), attn_out_norm=row(attn_out_norm),
        w_out_ssm=w_out[:ssm_w].astype(BF16), w_out_att=w_out[ssm_w:].astype(BF16),
        post_mix_norm=row(post_mix_norm), pre_mlp_norm=row(pre_mlp_norm),
        w_up=w_up.astype(BF16), w_down=w_down.astype(BF16), post_mlp_norm=row(post_mlp_norm),
        ssm_w=ssm_w, seg_len=seg_len,
    )


def _layer(x, p, rope_tables):
    ssm_w = p["ssm_w"]
    kv_w = N_KV_HEADS * HEAD_DIM
    attn_w = p["w_in"].shape[1] - ssm_w - 2 * kv_w
    seg_len = p["seg_len"]
    cos_tab, sin_tab = rope_tables
    u, q, k, v = _in_projection(x, p["pre_mix_norm"], p["w_in"], cos_tab, sin_tab,
                                p["q_norm"], p["k_norm"], ssm_w, attn_w, kv_w)
    u_perm = _segment_major(u, seg_len)
    yf, yb = _s5_scan(u_perm, p["b_re"], p["b_im"], p["c_re"], p["c_im"],
                      p["lam_re"], p["lam_im"], p["lamseg_re"], p["lamseg_im"], seg_len)
    y_ssm = _token_major(_glu(yf, yb, u_perm, p["ssm_d"], p["w_glu"], p["ssm_out_norm"]), seg_len)
    y_att = _attention(q, jnp.swapaxes(k, 1, 2), v)
    x1, h = _out_projection(x, y_ssm, y_att, p["attn_out_norm"], p["w_out_ssm"], p["w_out_att"],
                            p["post_mix_norm"], p["pre_mlp_norm"])
    return _mlp(h, x1, p["w_up"], p["w_down"], p["post_mlp_norm"])


def kernel(x_prompt, x_sample, pre_mix_norm, w_in, ssm_a_re, ssm_a_im, ssm_log_dt, ssm_b_re, ssm_b_im,
           ssm_c_re, ssm_c_im, ssm_d, w_glu, q_norm, k_norm, ssm_out_norm, attn_out_norm, w_out,
           post_mix_norm, pre_mlp_norm, w_up, w_down, post_mlp_norm):
    depth = w_in.shape[0]
    y_prompt, y_sample = x_prompt, x_sample
    rope_tables = _rope_tables(max(x_prompt.shape[1], x_sample.shape[1]))
    for l in range(depth):
        p = _prepare_weights(
            pre_mix_norm[l], w_in[l], ssm_a_re[l], ssm_a_im[l], ssm_log_dt[l], ssm_b_re[l], ssm_b_im[l],
            ssm_c_re[l], ssm_c_im[l], ssm_d[l], w_glu[l], q_norm[l], k_norm[l], ssm_out_norm[l],
            attn_out_norm[l], w_out[l], post_mix_norm[l], pre_mlp_norm[l], w_up[l], w_down[l],
            post_mlp_norm[l], SEG_LEN)
        y_prompt = _layer(y_prompt, p, rope_tables)
        y_sample = _layer(y_sample, p, rope_tables)
    return (y_prompt, y_sample)
```
